```python
import jax, jax.numpy as jnp
from jax import lax
import numpy as np

D_MODEL = 1024
BATCH = 16
SEQ = 2048
DEPTH = 4

HEAD_DIM = 64
CONV_CH = D_MODEL // 2
SB_HEADS = (D_MODEL // 2) // HEAD_DIM
FOX_HEADS = D_MODEL // HEAD_DIM
CONV_WIDTH = 31
D_FF = 256 * ((8 * D_MODEL // 3 + 255) // 256)
Q_BLOCK = 128
N_EVEN = (DEPTH + 1) // 2
N_ODD = DEPTH // 2
EPS = 1e-6
AB_IN = 2 * CONV_CH + 3 * SB_HEADS * HEAD_DIM
C_IN = 3 * FOX_HEADS * HEAD_DIM + FOX_HEADS

kernel_name = "hybrid_conv_stickbreak_fox_macaron"


def rms_norm(x, g):
    x32 = x.astype(jnp.float32)
    y = x32 * lax.rsqrt(jnp.mean(x32 * x32, axis=-1, keepdims=True) + EPS)
    return (y * g.astype(jnp.float32)).astype(x.dtype)


def layer_norm(x, g, b):
    x32 = x.astype(jnp.float32)
    mu = jnp.mean(x32, axis=-1, keepdims=True)
    xc = x32 - mu
    y = xc * lax.rsqrt(jnp.mean(xc * xc, axis=-1, keepdims=True) + EPS)
    return (y * g.astype(jnp.float32) + b.astype(jnp.float32)).astype(x.dtype)


def swiglu_ffn(h, w_in, w_out):
    g, u = jnp.split(h @ w_in, 2, axis=-1)
    return (jax.nn.silu(g) * u) @ w_out


def causal_depthwise_conv(u, w, b):
    y = lax.conv_general_dilated(
        u, w[:, None, :].astype(u.dtype), window_strides=(1,),
        padding=[(CONV_WIDTH - 1, 0)], dimension_numbers=('NWC', 'WIO', 'NWC'),
        feature_group_count=u.shape[-1])
    return y + b


def stick_breaking_attention(q, k, v):
    seq = q.shape[2]
    scale = HEAD_DIM ** -0.5
    outs = []
    for i in range(seq // Q_BLOCK):
        t0, t1 = i * Q_BLOCK, (i + 1) * Q_BLOCK
        z = jnp.einsum('bhqd,bhkd->bhqk', q[:, :, t0:t1], k[:, :, :t1]).astype(jnp.float32) * scale
        strict = jnp.arange(t1)[None, :] < jnp.arange(t0, t1)[:, None]
        log_fail = jnp.where(strict, -jax.nn.softplus(z), 0.0)
        between = lax.cumsum(log_fail, axis=3, reverse=True) - log_fail
        w = jnp.where(strict, jnp.exp(jax.nn.log_sigmoid(z) + between), 0.0)
        outs.append(jnp.einsum('bhqk,bhkd->bhqd', w.astype(v.dtype), v[:, :, :t1]))
    return jnp.concatenate(outs, axis=2)


def forgetting_attention(q, k, v, log_f):
    seq = q.shape[2]
    scale = HEAD_DIM ** -0.5
    cum = lax.cumsum(log_f, axis=2)
    outs = []
    for i in range(seq // Q_BLOCK):
        t0, t1 = i * Q_BLOCK, (i + 1) * Q_BLOCK
        z = jnp.einsum('bhqd,bhkd->bhqk', q[:, :, t0:t1], k[:, :, :t1]).astype(jnp.float32) * scale
        z = z + cum[:, :, t0:t1, None] - cum[:, :, None, :t1]
        causal = jnp.arange(t1)[None, :] <= jnp.arange(t0, t1)[:, None]
        p = jax.nn.softmax(jnp.where(causal, z, -jnp.inf), axis=-1)
        outs.append(jnp.einsum('bhqk,bhkd->bhqd', p.astype(v.dtype), v[:, :, :t1]))
    return jnp.concatenate(outs, axis=2)


def conv_sb_mixer(h, w_in, conv_w, conv_b, ln_g, ln_b, w_out):
    b, s, _ = h.shape
    proj = h @ w_in
    a_val = proj[..., :CONV_CH]
    a_gate = proj[..., CONV_CH:2 * CONV_CH]
    qkv = proj[..., 2 * CONV_CH:]
    u = a_val * jax.nn.sigmoid(a_gate)
    u = causal_depthwise_conv(u, conv_w, conv_b)
    u = jax.nn.silu(layer_norm(u, ln_g, ln_b))
    qkv = qkv.reshape(b, s, 3, SB_HEADS, HEAD_DIM).transpose(2, 0, 3, 1, 4)
    o = stick_breaking_attention(qkv[0], qkv[1], qkv[2])
    o = o.transpose(0, 2, 1, 3).reshape(b, s, SB_HEADS * HEAD_DIM)
    return jnp.concatenate([u, o], axis=-1) @ w_out


def fox_mixer(h, w_in, f_bias, q_g, k_g, w_out):
    b, s, _ = h.shape
    proj = h @ w_in
    hd = FOX_HEADS * HEAD_DIM
    qkv = proj[..., :3 * hd].reshape(b, s, 3, FOX_HEADS, HEAD_DIM).transpose(2, 0, 3, 1, 4)
    q = rms_norm(qkv[0], q_g)
    k = rms_norm(qkv[1], k_g)
    log_f = jax.nn.log_sigmoid((proj[..., 3 * hd:] + f_bias).astype(jnp.float32)).transpose(0, 2, 1)
    o = forgetting_attention(q, k, qkv[2], log_f)
    o = o.transpose(0, 2, 1, 3).reshape(b, s, hd)
    return o @ w_out


def setup_inputs(seed: int = 0) -> dict:
    key = jax.random.key(seed)
    ks = jax.random.split(key, 20)
    f32 = jnp.float32

    def w(k, shape, fan_in):
        return jax.random.normal(k, shape, f32) * (fan_in ** -0.5)

    def gain(k, shape):
        return 1.0 + 0.02 * jax.random.normal(k, shape, f32)

    def bias(k, shape):
        return 0.02 * jax.random.normal(k, shape, f32)

    return {
        "x": jax.random.normal(ks[0], (BATCH, SEQ, D_MODEL), f32),
        "ffn1_norm": gain(ks[1], (DEPTH, D_MODEL)),
        "ffn1_w_in": w(ks[2], (DEPTH, D_MODEL, 2 * D_FF), D_MODEL),
        "ffn1_w_out": w(ks[3], (DEPTH, D_FF, D_MODEL), D_FF),
        "mix_norm": gain(ks[4], (DEPTH, D_MODEL)),
        "ffn2_norm": gain(ks[5], (DEPTH, D_MODEL)),
        "ffn2_w_in": w(ks[6], (DEPTH, D_MODEL, 2 * D_FF), D_MODEL),
        "ffn2_w_out": w(ks[7], (DEPTH, D_FF, D_MODEL), D_FF),
        "ab_w_in": w(ks[8], (N_EVEN, D_MODEL, AB_IN), D_MODEL),
        "conv_w": w(ks[9], (N_EVEN, CONV_WIDTH, CONV_CH), CONV_WIDTH),
        "conv_b": bias(ks[10], (N_EVEN, CONV_CH)),
        "conv_ln_g": gain(ks[11], (N_EVEN, CONV_CH)),
        "conv_ln_b": bias(ks[12], (N_EVEN, CONV_CH)),
        "ab_w_out": w(ks[13], (N_EVEN, CONV_CH + SB_HEADS * HEAD_DIM, D_MODEL), CONV_CH + SB_HEADS * HEAD_DIM),
        "fox_w_in": w(ks[14], (N_ODD, D_MODEL, C_IN), D_MODEL),
        "fox_f_bias": jax.random.uniform(ks[15], (N_ODD, FOX_HEADS), f32, 1.0, 4.0),
        "fox_q_norm": gain(ks[16], (N_ODD, HEAD_DIM)),
        "fox_k_norm": gain(ks[17], (N_ODD, HEAD_DIM)),
        "fox_w_out": w(ks[18], (N_ODD, FOX_HEADS * HEAD_DIM, D_MODEL), FOX_HEADS * HEAD_DIM),
    }


def reference(x, ffn1_norm, ffn1_w_in, ffn1_w_out, mix_norm, ffn2_norm, ffn2_w_in, ffn2_w_out,
              ab_w_in, conv_w, conv_b, conv_ln_g, conv_ln_b, ab_w_out,
              fox_w_in, fox_f_bias, fox_q_norm, fox_k_norm, fox_w_out):
    for layer in range(DEPTH):
        x = x + 0.5 * swiglu_ffn(rms_norm(x, ffn1_norm[layer]), ffn1_w_in[layer], ffn1_w_out[layer])
        h = rms_norm(x, mix_norm[layer])
        if layer % 2 == 0:
            e = layer // 2
            x = x + conv_sb_mixer(h, ab_w_in[e], conv_w[e], conv_b[e], conv_ln_g[e], conv_ln_b[e], ab_w_out[e])
        else:
            o = layer // 2
            x = x + fox_mixer(h, fox_w_in[o], fox_f_bias[o], fox_q_norm[o], fox_k_norm[o], fox_w_out[o])
        x = x + 0.5 * swiglu_ffn(rms_norm(x, ffn2_norm[layer]), ffn2_w_in[layer], ffn2_w_out[layer])
    return x
```

```python
import functools

import jax
import jax.numpy as jnp
from jax import lax
from jax.experimental import pallas as pl
from jax.experimental.pallas import tpu as pltpu

F32 = jnp.float32
BF16 = jnp.bfloat16

EPS = 1e-6
HEAD_DIM = 64
CONV_WIDTH = 31
ATTN_SCALE = HEAD_DIM ** -0.5

V7X_LANES = 128
V7X_SUBLANES = 8
V7X_MXU_DIM = 256
V7X_VMEM_BYTES = 64 * 1024 * 1024

HEADS_PER_GROUP = V7X_LANES // HEAD_DIM
CONV_HALO = 32

_NT = (((1,), (1,)), ((), ()))


def _nbytes(shape, dtype):
    n = 1
    for d in shape:
        n *= d
    return n * jnp.dtype(dtype).itemsize


def _compiler_params(semantics, block_bytes, scratch_bytes, temp_bytes):
    need = 2 * block_bytes + scratch_bytes + temp_bytes
    mib = 1024 * 1024
    limit = min(-(-need // mib) * mib, V7X_VMEM_BYTES - 4 * mib)
    return pltpu.CompilerParams(dimension_semantics=semantics, vmem_limit_bytes=limit)


def _dot(a, b):
    return jnp.dot(a, b, preferred_element_type=F32)


def _rms_to_bf16(x, g):
    ms = jnp.mean(x * x, axis=-1, keepdims=True)
    return ((x * lax.rsqrt(ms + EPS)) * g).astype(BF16)


def _split2(x):
    hi = x.astype(BF16)
    lo = (x - hi.astype(F32)).astype(BF16)
    return hi, lo


def _softplus(z):
    return jnp.maximum(z, 0.0) + jnp.log(1.0 + jnp.exp(-jnp.abs(z)))


def _ffn_body(x_ref, g_ref, wg_ref, wu_ref, wo_ref, o_ref, h_ref):
    j = pl.program_id(1)

    @pl.when(j == 0)
    def _():
        h_ref[...] = _rms_to_bf16(x_ref[...], g_ref[...])

    h = h_ref[...]
    gate = _dot(h, wg_ref[...])
    up = _dot(h, wu_ref[...])
    act = (gate * jax.nn.sigmoid(gate) * up).astype(BF16)
    half = 0.5 * _dot(act, wo_ref[...])

    @pl.when(j == 0)
    def _():
        o_ref[...] = x_ref[...] + half

    @pl.when(j != 0)
    def _():
        o_ref[...] += half


def _ffn(x, norm, w_in, w_out, layer, *, tm, tf):
    n, d = x.shape
    d_ff = w_out.shape[1]
    nj = d_ff // tf
    blocks = (_nbytes((tm, d), F32) * 2 + _nbytes((d, tf), BF16) * 3 + _nbytes((1, d), F32))
    temps = 3 * _nbytes((tm, tf), F32) + _nbytes((tm, d), F32)
    return pl.pallas_call(
        _ffn_body,
        grid=(n // tm, nj),
        in_specs=[
            pl.BlockSpec((tm, d), lambda i, j: (i, 0)),
            pl.BlockSpec((None, 1, d), lambda i, j: (layer, 0, 0)),
            pl.BlockSpec((None, d, tf), lambda i, j: (layer, 0, j)),
            pl.BlockSpec((None, d, tf), lambda i, j: (layer, 0, j + nj)),
            pl.BlockSpec((None, tf, d), lambda i, j: (layer, j, 0)),
        ],
        out_specs=pl.BlockSpec((tm, d), lambda i, j: (i, 0)),
        out_shape=jax.ShapeDtypeStruct((n, d), F32),
        scratch_shapes=[pltpu.VMEM((tm, d), BF16)],
        compiler_params=_compiler_params(("parallel", "arbitrary"), blocks, _nbytes((tm, d), BF16), temps),
        name="ffn",
    )(x, norm, w_in, w_in, w_out)


def _ab_in_body(x_ref, g_ref, w_ref, u_ref, q_ref, k_ref, v_ref, *, c):
    h = _rms_to_bf16(x_ref[...], g_ref[...])
    a_val = _dot(h, w_ref[:, 0:c])
    a_gate = _dot(h, w_ref[:, c:2 * c])
    u_ref[...] = a_val * jax.nn.sigmoid(a_gate)
    q_ref[...] = (_dot(h, w_ref[:, 2 * c:3 * c]) * ATTN_SCALE).astype(BF16)
    k_ref[...] = _dot(h, w_ref[:, 3 * c:4 * c]).astype(BF16)
    v_ref[...] = _dot(h, w_ref[:, 4 * c:5 * c]).astype(BF16)


def _ab_in(x, norm, w, layer, e, *, tm):
    n, d = x.shape
    width = w.shape[2]
    c = width // 5
    row = lambda i: (i, 0)
    blocks = (_nbytes((tm, d), F32) + _nbytes((1, d), F32) + _nbytes((d, width), BF16)
              + _nbytes((tm, c), F32) + 3 * _nbytes((tm, c), BF16))
    temps = 2 * _nbytes((tm, c), F32) + _nbytes((tm, d), BF16)
    return pl.pallas_call(
        functools.partial(_ab_in_body, c=c),
        grid=(n // tm,),
        in_specs=[
            pl.BlockSpec((tm, d), row),
            pl.BlockSpec((None, 1, d), lambda i: (layer, 0, 0)),
            pl.BlockSpec((None, d, width), lambda i: (e, 0, 0)),
        ],
        out_specs=[pl.BlockSpec((tm, c), row)] * 4,
        out_shape=[jax.ShapeDtypeStruct((n, c), F32)] + [jax.ShapeDtypeStruct((n, c), BF16)] * 3,
        compiler_params=_compiler_params(("parallel",), blocks, 0, temps),
        name="ab_in",
    )(x, norm, w)


def _conv_body(halo_ref, u_ref, w_ref, b_ref, lg_ref, lb_ref, o_ref, buf_ref, *, tc, rows):
    s = pl.program_id(1)
    c = u_ref.shape[-1]
    buf_ref[0:CONV_HALO, :] = jnp.where(s == 0, 0.0, halo_ref[0])
    buf_ref[CONV_HALO:, :] = u_ref[0]
    first = CONV_HALO - (CONV_WIDTH - 1)
    for r in range(tc // rows):
        acc = jnp.broadcast_to(b_ref[...], (rows, c))
        for tap in range(CONV_WIDTH):
            start = r * rows + first + tap
            acc = acc + w_ref[tap:tap + 1, :] * buf_ref[start:start + rows, :]
        mu = jnp.mean(acc, axis=-1, keepdims=True)
        xc = acc - mu
        var = jnp.mean(xc * xc, axis=-1, keepdims=True)
        y = xc * lax.rsqrt(var + EPS) * lg_ref[...] + lb_ref[...]
        o_ref[0, r * rows:(r + 1) * rows, :] = (y * jax.nn.sigmoid(y)).astype(BF16)


def _conv_ln_swish(u, w, b, lg, lb, e, *, tc, rows):
    bsz, s, c = u.shape
    per = tc // CONV_HALO
    vec = pl.BlockSpec((None, 1, c), lambda bi, si: (e, 0, 0))
    blocks = (_nbytes((CONV_HALO, c), F32) + _nbytes((tc, c), F32) + _nbytes((CONV_WIDTH + 1, c), F32)
              + 3 * _nbytes((V7X_SUBLANES, c), F32) + _nbytes((tc, c), BF16))
    scratch = _nbytes((tc + CONV_HALO, c), F32)
    return pl.pallas_call(
        functools.partial(_conv_body, tc=tc, rows=rows),
        grid=(bsz, s // tc),
        in_specs=[
            pl.BlockSpec((1, CONV_HALO, c), lambda bi, si: (bi, jnp.maximum(si * per - 1, 0), 0)),
            pl.BlockSpec((1, tc, c), lambda bi, si: (bi, si, 0)),
            pl.BlockSpec((None, CONV_WIDTH, c), lambda bi, si: (e, 0, 0)),
            vec, vec, vec,
        ],
        out_specs=pl.BlockSpec((1, tc, c), lambda bi, si: (bi, si, 0)),
        out_shape=jax.ShapeDtypeStruct((bsz, s, c), BF16),
        scratch_shapes=[pltpu.VMEM((tc + CONV_HALO, c), F32)],
        compiler_params=_compiler_params(("parallel", "parallel"), blocks, scratch,
                                         (tc // rows) * CONV_WIDTH * _nbytes((rows, c), F32)),
        name="conv_ln_swish",
    )(u, u, w, b, lg, lb)


def _head_masks(q):
    lane = lax.broadcasted_iota(jnp.int32, q.shape, 1)
    low = lane < HEAD_DIM
    zero = jnp.zeros_like(q)
    return low, (jnp.where(low, q, zero), jnp.where(low, zero, q))


def _sb_body(q_ref, k_ref, v_ref, tri_ref, o_ref, *, tq, tk):
    i = pl.program_id(2)
    low, q_heads = _head_masks(q_ref[0])
    tri = tri_ref[...]
    t0 = i * tq
    diag = (t0 + tq - 1) // tk

    def tile(kb, carry, on_diagonal):
        s0 = pl.multiple_of(kb * tk, tk)
        kblk = k_ref[0, pl.ds(s0, tk), :]
        vblk = v_ref[0, pl.ds(s0, tk), :]
        if on_diagonal:
            row = lax.broadcasted_iota(jnp.int32, (tq, tk), 0) + t0
            col = lax.broadcasted_iota(jnp.int32, (tq, tk), 1) + s0
            strict = col < row
        out = []
        for hh in range(HEADS_PER_GROUP):
            acc, run = carry[hh]
            z = lax.dot_general(q_heads[hh], kblk, _NT, preferred_element_type=F32)
            sp = _softplus(z)
            if on_diagonal:
                sp = jnp.where(strict, sp, 0.0)
            hi, lo = _split2(sp)
            after = _dot(hi, tri) + _dot(lo, tri)
            w = jnp.exp(z - sp - after - run)
            if on_diagonal:
                w = jnp.where(strict, w, 0.0)
            acc = acc + _dot(w.astype(BF16), vblk)
            run = run + (after[:, 0:1] + sp[:, 0:1])
            out.append((acc, run))
        return tuple(out)

    init = tuple((jnp.zeros((tq, V7X_LANES), F32), jnp.zeros((tq, 1), F32)) for _ in range(HEADS_PER_GROUP))
    carry = tile(diag, init, True)
    carry = lax.fori_loop(0, diag, lambda n, c: tile(diag - 1 - n, c, False), carry)
    o_ref[0] = jnp.where(low, carry[0][0], carry[1][0]).astype(BF16)


def _strict_lower_ones(n):
    j = lax.broadcasted_iota(jnp.int32, (n, n), 0)
    s = lax.broadcasted_iota(jnp.int32, (n, n), 1)
    return (j > s).astype(BF16)


def _sb_attention(q, k, v, *, tq, tk):
    bsz, s, hd = q.shape
    groups = hd // V7X_LANES
    assert tk % tq == 0 and s % tk == 0
    qspec = pl.BlockSpec((1, tq, V7X_LANES), lambda b, p, i: (b, i, p))
    kvspec = pl.BlockSpec((1, s, V7X_LANES), lambda b, p, i: (b, 0, p))
    blocks = (2 * _nbytes((tq, V7X_LANES), BF16) + 2 * _nbytes((s, V7X_LANES), BF16) + _nbytes((tk, tk), BF16))
    return pl.pallas_call(
        functools.partial(_sb_body, tq=tq, tk=tk),
        grid=(bsz, groups, s // tq),
        in_specs=[qspec, kvspec, kvspec, pl.BlockSpec((tk, tk), lambda b, p, i: (0, 0))],
        out_specs=qspec,
        out_shape=jax.ShapeDtypeStruct((bsz, s, hd), BF16),
        compiler_params=_compiler_params(("parallel", "parallel", "parallel"), blocks, 0,
                                         16 * _nbytes((tq, tk), F32)),
        name="sb_attention",
    )(q, k, v, _strict_lower_ones(tk))


def _out_proj_body(*refs, widths):
    x_ref = refs[0]
    a_refs = refs[1:1 + len(widths)]
    w_ref = refs[1 + len(widths)]
    o_ref = refs[2 + len(widths)]
    acc = x_ref[...]
    off = 0
    for a_ref, width in zip(a_refs, widths):
        acc = acc + _dot(a_ref[...], w_ref[off:off + width, :])
        off += width
    o_ref[...] = acc


def _out_proj(x, parts, w, e, *, tm):
    n, d = x.shape
    widths = tuple(p.shape[1] for p in parts)
    row = lambda i: (i, 0)
    blocks = (2 * _nbytes((tm, d), F32) + sum(_nbytes((tm, wd), BF16) for wd in widths)
              + _nbytes((sum(widths), d), BF16))
    return pl.pallas_call(
        functools.partial(_out_proj_body, widths=widths),
        grid=(n // tm,),
        in_specs=[pl.BlockSpec((tm, d), row)] + [pl.BlockSpec((tm, wd), row) for wd in widths]
                 + [pl.BlockSpec((None, sum(widths), d), lambda i: (e, 0, 0))],
        out_specs=pl.BlockSpec((tm, d), row),
        out_shape=jax.ShapeDtypeStruct((n, d), F32),
        compiler_params=_compiler_params(("parallel",), blocks, 0, _nbytes((tm, d), F32)),
        name="out_proj",
    )(x, *parts, w)


def _fox_in_body(x_ref, g_ref, w_ref, wft_ref, fb_ref, qg_ref, kg_ref, bd_ref, tri_ref,
                 q_ref, k_ref, v_ref, cum_ref, carry_ref, *, tm, hd):
    s = pl.program_id(1)
    h = _rms_to_bf16(x_ref[0], g_ref[...])
    bd = bd_ref[...]
    inv_dim = 1.0 / HEAD_DIM
    for grp in range(hd // V7X_LANES):
        lanes = slice(grp * V7X_LANES, (grp + 1) * V7X_LANES)
        for base, gain_ref, out_ref, scale in ((0, qg_ref, q_ref, ATTN_SCALE), (hd, kg_ref, k_ref, 1.0)):
            xg = _dot(h, w_ref[:, base + lanes.start:base + lanes.stop])
            hi, lo = _split2(xg * xg)
            ss = _dot(hi, bd) + _dot(lo, bd)
            y = (xg * lax.rsqrt(ss * inv_dim + EPS)) * gain_ref[...]
            out_ref[0, :, lanes] = (y * scale).astype(BF16)
    v_ref[0] = _dot(h, w_ref[:, 2 * hd:3 * hd]).astype(BF16)

    xf = lax.dot_general(wft_ref[...], h, _NT, preferred_element_type=F32) + fb_ref[...]
    logf = jnp.minimum(xf, 0.0) - jnp.log(1.0 + jnp.exp(-jnp.abs(xf)))
    p0 = logf.astype(BF16)
    r0 = logf - p0.astype(F32)
    p1 = r0.astype(BF16)
    p2 = (r0 - p1.astype(F32)).astype(BF16)
    tri = tri_ref[...]
    local = _dot(p0, tri) + _dot(p1, tri) + _dot(p2, tri)

    @pl.when(s == 0)
    def _():
        carry_ref[...] = jnp.zeros_like(carry_ref)

    cum = local + carry_ref[...]
    cum_ref[0] = cum
    carry_ref[...] = cum[:, tm - 1:tm]


def _head_block_ones():
    a = lax.broadcasted_iota(jnp.int32, (V7X_LANES, V7X_LANES), 0) // HEAD_DIM
    b = lax.broadcasted_iota(jnp.int32, (V7X_LANES, V7X_LANES), 1) // HEAD_DIM
    return (a == b).astype(BF16)


def _inclusive_upper_ones(n):
    j = lax.broadcasted_iota(jnp.int32, (n, n), 0)
    s = lax.broadcasted_iota(jnp.int32, (n, n), 1)
    return (j <= s).astype(BF16)


def _fox_in(x, norm, w, wft, fb, qg, kg, layer, o, *, tm):
    bsz, s, d = x.shape
    hd = w.shape[2] // 3
    heads = wft.shape[1]
    tok = lambda b, si: (b, si, 0)
    sel = lambda b, si: (o, 0, 0)
    blocks = (_nbytes((tm, d), F32) + _nbytes((d, 3 * hd), BF16) + _nbytes((heads, d), BF16)
              + 3 * _nbytes((tm, hd), BF16) + _nbytes((heads, tm), F32) + _nbytes((tm, tm), BF16)
              + _nbytes((V7X_LANES, V7X_LANES), BF16) + 4 * _nbytes((V7X_SUBLANES, d), F32))
    temps = _nbytes((tm, d), BF16) + _nbytes((tm, hd), F32) + 6 * _nbytes((tm, V7X_LANES), F32)
    return pl.pallas_call(
        functools.partial(_fox_in_body, tm=tm, hd=hd),
        grid=(bsz, s // tm),
        in_specs=[
            pl.BlockSpec((1, tm, d), tok),
            pl.BlockSpec((None, 1, d), lambda b, si: (layer, 0, 0)),
            pl.BlockSpec((None, d, 3 * hd), sel),
            pl.BlockSpec((None, heads, d), sel),
            pl.BlockSpec((None, heads, 1), sel),
            pl.BlockSpec((None, 1, V7X_LANES), sel),
            pl.BlockSpec((None, 1, V7X_LANES), sel),
            pl.BlockSpec((V7X_LANES, V7X_LANES), lambda b, si: (0, 0)),
            pl.BlockSpec((tm, tm), lambda b, si: (0, 0)),
        ],
        out_specs=[pl.BlockSpec((1, tm, hd), tok)] * 3 + [pl.BlockSpec((1, heads, tm), lambda b, si: (b, 0, si))],
        out_shape=[jax.ShapeDtypeStruct((bsz, s, hd), BF16)] * 3 + [jax.ShapeDtypeStruct((bsz, heads, s), F32)],
        scratch_shapes=[pltpu.VMEM((heads, 1), F32)],
        compiler_params=_compiler_params(("parallel", "arbitrary"), blocks,
                                         _nbytes((heads, V7X_LANES), F32), temps),
        name="fox_in",
    )(x, norm, w, wft, fb, qg, kg, _head_block_ones(), _inclusive_upper_ones(tm))


def _fox_body(q_ref, k_ref, v_ref, cum_ref, o_ref, *, tq, tk):
    p = pl.program_id(1)
    i = pl.program_id(2)
    low, q_heads = _head_masks(q_ref[0])
    t0 = i * tq
    diag = (t0 + tq - 1) // tk

    def scores(kb, hh):
        s0 = pl.multiple_of(kb * tk, tk)
        kblk = k_ref[0, pl.ds(s0, tk), :]
        decay = cum_ref[0, pl.ds(p * HEADS_PER_GROUP + hh, 1), pl.ds(s0, tk)]
        z = lax.dot_general(q_heads[hh], kblk, _NT, preferred_element_type=F32) - decay
        return z, v_ref[0, pl.ds(s0, tk), :], s0

    state = []
    for hh in range(HEADS_PER_GROUP):
        z, vblk, s0 = scores(diag, hh)
        row = lax.broadcasted_iota(jnp.int32, (tq, tk), 0) + t0
        col = lax.broadcasted_iota(jnp.int32, (tq, tk), 1) + s0
        z = jnp.where(col <= row, z, -jnp.inf)
        m = jnp.max(z, axis=1, keepdims=True)
        pr = jnp.exp(z - m)
        state.append((m, jnp.sum(pr, axis=1, keepdims=True), _dot(pr.astype(BF16), vblk)))

    def step(n, state):
        out = []
        for hh in range(HEADS_PER_GROUP):
            m, l, acc = state[hh]
            z, vblk, _ = scores(diag - 1 - n, hh)
            m_new = jnp.maximum(m, jnp.max(z, axis=1, keepdims=True))
            alpha = jnp.exp(m - m_new)
            pr = jnp.exp(z - m_new)
            l = alpha * l + jnp.sum(pr, axis=1, keepdims=True)
            acc = alpha * acc + _dot(pr.astype(BF16), vblk)
            out.append((m_new, l, acc))
        return tuple(out)

    state = lax.fori_loop(0, diag, step, tuple(state))
    outs = [acc / l for (_, l, acc) in state]
    o_ref[0] = jnp.where(low, outs[0], outs[1]).astype(BF16)


def _fox_attention(q, k, v, cum, *, tq, tk):
    bsz, s, hd = q.shape
    heads = cum.shape[1]
    groups = hd // V7X_LANES
    assert tk % tq == 0 and s % tk == 0
    qspec = pl.BlockSpec((1, tq, V7X_LANES), lambda b, p, i: (b, i, p))
    kvspec = pl.BlockSpec((1, s, V7X_LANES), lambda b, p, i: (b, 0, p))
    blocks = (2 * _nbytes((tq, V7X_LANES), BF16) + 2 * _nbytes((s, V7X_LANES), BF16) + _nbytes((heads, s), F32))
    return pl.pallas_call(
        functools.partial(_fox_body, tq=tq, tk=tk),
        grid=(bsz, groups, s // tq),
        in_specs=[qspec, kvspec, kvspec, pl.BlockSpec((1, heads, s), lambda b, p, i: (b, 0, 0))],
        out_specs=qspec,
        out_shape=jax.ShapeDtypeStruct((bsz, s, hd), BF16),
        compiler_params=_compiler_params(("parallel", "parallel", "parallel"), blocks, 0,
                                         12 * _nbytes((tq, tk), F32)),
        name="fox_attention",
    )(q, k, v, cum)


FFN_ROWS = 512
FFN_COLS = 1408
CONV_ROWS = 256
CONV_ACC_ROWS = 32
ATTN_Q = 128
ATTN_K = V7X_MXU_DIM


def kernel(x, ffn1_norm, ffn1_w_in, ffn1_w_out, mix_norm, ffn2_norm, ffn2_w_in, ffn2_w_out, ab_w_in, conv_w, conv_b,
           conv_ln_g, conv_ln_b, ab_w_out, fox_w_in, fox_f_bias, fox_q_norm, fox_k_norm, fox_w_out):
    bsz, s, d = x.shape
    depth = ffn1_norm.shape[0]
    n = bsz * s
    row3 = lambda a: a.reshape(a.shape[0], 1, a.shape[1])

    ffn1_norm, ffn2_norm, mix_norm = row3(ffn1_norm), row3(ffn2_norm), row3(mix_norm)
    conv_b, conv_ln_g, conv_ln_b = row3(conv_b), row3(conv_ln_g), row3(conv_ln_b)
    ffn1_w_in, ffn1_w_out = ffn1_w_in.astype(BF16), ffn1_w_out.astype(BF16)
    ffn2_w_in, ffn2_w_out = ffn2_w_in.astype(BF16), ffn2_w_out.astype(BF16)
    ab_w_in, ab_w_out = ab_w_in.astype(BF16), ab_w_out.astype(BF16)
    fox_w_out = fox_w_out.astype(BF16)
    hd = fox_w_out.shape[1]
    fox_w_qkv = fox_w_in[:, :, :3 * hd].astype(BF16)
    fox_w_ft = jnp.swapaxes(fox_w_in[:, :, 3 * hd:], 1, 2).astype(BF16)
    fox_fb = fox_f_bias[:, :, None]
    fox_qg = row3(jnp.tile(fox_q_norm, (1, HEADS_PER_GROUP)))
    fox_kg = row3(jnp.tile(fox_k_norm, (1, HEADS_PER_GROUP)))

    xf = x.reshape(n, d)
    for layer in range(depth):
        xf = _ffn(xf, ffn1_norm, ffn1_w_in, ffn1_w_out, layer, tm=FFN_ROWS, tf=FFN_COLS)
        if layer % 2 == 0:
            e = layer // 2
            u, q, k, v = _ab_in(xf, mix_norm, ab_w_in, layer, e, tm=FFN_ROWS)
            c = u.shape[1]
            u = _conv_ln_swish(u.reshape(bsz, s, c), conv_w, conv_b, conv_ln_g, conv_ln_b, e,
                               tc=CONV_ROWS, rows=CONV_ACC_ROWS)
            att = _sb_attention(q.reshape(bsz, s, c), k.reshape(bsz, s, c), v.reshape(bsz, s, c),
                                tq=ATTN_Q, tk=ATTN_K)
            xf = _out_proj(xf, [u.reshape(n, c), att.reshape(n, c)], ab_w_out, e, tm=FFN_ROWS)
        else:
            o = layer // 2
            q, k, v, cum = _fox_in(xf.reshape(bsz, s, d), mix_norm, fox_w_qkv, fox_w_ft, fox_fb, fox_qg, fox_kg,
                                   layer, o, tm=FFN_ROWS)
            att = _fox_attention(q, k, v, cum, tq=ATTN_Q, tk=ATTN_K)
            xf = _out_proj(xf, [att.reshape(n, hd)], fox_w_out, o, tm=FFN_ROWS)
        xf = _ffn(xf, ffn2_norm, ffn2_w_in, ffn2_w_out, layer, tm=FFN_ROWS, tf=FFN_COLS)
    return xf.reshape(bsz, s, d)
```

```python
import functools

import jax
import jax.numpy as jnp
from jax import lax
from jax.experimental import pallas as pl
from jax.experimental.pallas import tpu as pltpu

F32 = jnp.float32
BF16 = jnp.bfloat16

EPS = 1e-6
HEAD_DIM = 64
CONV_WIDTH = 31
ATTN_SCALE = HEAD_DIM ** -0.5

V7X_LANES = 128
V7X_SUBLANES = 8
V7X_MXU_DIM = 256
V7X_VMEM_BYTES = 64 * 1024 * 1024

HEADS_PER_GROUP = V7X_LANES // HEAD_DIM
CONV_HALO = 32

_NT = (((1,), (1,)), ((), ()))


def _nbytes(shape, dtype):
    n = 1
    for d in shape:
        n *= d
    return n * jnp.dtype(dtype).itemsize


def _compiler_params(semantics, block_bytes, scratch_bytes, temp_bytes):
    need = 2 * block_bytes + scratch_bytes + temp_bytes
    mib = 1024 * 1024
    limit = min(-(-need // mib) * mib, V7X_VMEM_BYTES - 4 * mib)
    return pltpu.CompilerParams(dimension_semantics=semantics, vmem_limit_bytes=limit)


def _dot(a, b):
    return jnp.dot(a, b, preferred_element_type=F32)


def _rms_to_bf16(x, g):
    ms = jnp.mean(x * x, axis=-1, keepdims=True)
    return ((x * lax.rsqrt(ms + EPS)) * g).astype(BF16)


def _split2(x):
    hi = x.astype(BF16)
    lo = (x - hi.astype(F32)).astype(BF16)
    return hi, lo


def _softplus(z):
    return jnp.maximum(z, 0.0) + jnp.log(1.0 + jnp.exp(-jnp.abs(z)))


def _ffn_body(x_ref, g_ref, wg_ref, wu_ref, wo_ref, o_ref):
    x = x_ref[...]
    h = _rms_to_bf16(x, g_ref[...])
    gate = _dot(h, wg_ref[...])
    up = _dot(h, wu_ref[...])
    act = (gate * jax.nn.sigmoid(gate) * up).astype(BF16)
    o_ref[...] = x + 0.5 * _dot(act, wo_ref[...])


def _resident(block_shape, index_map):
    return pl.BlockSpec(block_shape, index_map, pipeline_mode=pl.Buffered(1))


def _ffn(x, norm, w_in, w_out, layer, *, tm):
    n, d = x.shape
    d_ff = w_out.shape[1]
    blocks = _nbytes((tm, d), F32) * 2 + _nbytes((1, d), F32)
    weights = _nbytes((d, d_ff), BF16) * 3
    temps = 2 * _nbytes((tm, d_ff), F32) + _nbytes((tm, d_ff), BF16) + _nbytes((tm, d), F32)
    return pl.pallas_call(
        _ffn_body,
        grid=(n // tm,),
        in_specs=[
            pl.BlockSpec((tm, d), lambda i: (i, 0)),
            pl.BlockSpec((None, 1, d), lambda i: (layer, 0, 0)),
            _resident((None, d, d_ff), lambda i: (layer, 0, 0)),
            _resident((None, d, d_ff), lambda i: (layer, 0, 1)),
            _resident((None, d_ff, d), lambda i: (layer, 0, 0)),
        ],
        out_specs=pl.BlockSpec((tm, d), lambda i: (i, 0)),
        out_shape=jax.ShapeDtypeStruct((n, d), F32),
        compiler_params=_compiler_params(("parallel",), blocks, weights, temps),
        name="ffn",
    )(x, norm, w_in, w_in, w_out)


def _ab_in_body(x_ref, g_ref, w_ref, u_ref, q_ref, k_ref, v_ref, *, c):
    h = _rms_to_bf16(x_ref[...], g_ref[...])
    a_val = _dot(h, w_ref[:, 0:c])
    a_gate = _dot(h, w_ref[:, c:2 * c])
    u_ref[...] = a_val * jax.nn.sigmoid(a_gate)
    q_ref[...] = (_dot(h, w_ref[:, 2 * c:3 * c]) * ATTN_SCALE).astype(BF16)
    k_ref[...] = _dot(h, w_ref[:, 3 * c:4 * c]).astype(BF16)
    v_ref[...] = _dot(h, w_ref[:, 4 * c:5 * c]).astype(BF16)


def _ab_in(x, norm, w, layer, e, *, tm):
    n, d = x.shape
    width = w.shape[2]
    c = width // 5
    row = lambda i: (i, 0)
    blocks = (_nbytes((tm, d), F32) + _nbytes((1, d), F32) + _nbytes((d, width), BF16)
              + _nbytes((tm, c), F32) + 3 * _nbytes((tm, c), BF16))
    temps = 2 * _nbytes((tm, c), F32) + _nbytes((tm, d), BF16)
    return pl.pallas_call(
        functools.partial(_ab_in_body, c=c),
        grid=(n // tm,),
        in_specs=[
            pl.BlockSpec((tm, d), row),
            pl.BlockSpec((None, 1, d), lambda i: (layer, 0, 0)),
            pl.BlockSpec((None, d, width), lambda i: (e, 0, 0)),
        ],
        out_specs=[pl.BlockSpec((tm, c), row)] * 4,
        out_shape=[jax.ShapeDtypeStruct((n, c), F32)] + [jax.ShapeDtypeStruct((n, c), BF16)] * 3,
        compiler_params=_compiler_params(("parallel",), blocks, 0, temps),
        name="ab_in",
    )(x, norm, w)


def _conv_body(halo_ref, u_ref, w_ref, b_ref, lg_ref, lb_ref, o_ref, buf_ref, *, tc, rows):
    s = pl.program_id(1)
    c = u_ref.shape[-1]
    buf_ref[0, 0:CONV_HALO, :] = jnp.where(s == 0, 0.0, halo_ref[0])
    buf_ref[0, CONV_HALO:, :] = u_ref[0]
    moved = tc + CONV_HALO - V7X_SUBLANES
    for p in range(1, V7X_SUBLANES):
        buf_ref[p, 0:moved, :] = buf_ref[0, p:p + moved, :]
    first = CONV_HALO - (CONV_WIDTH - 1)
    for r in range(tc // rows):
        acc = jnp.broadcast_to(b_ref[...], (rows, c))
        for tap in range(CONV_WIDTH):
            start = r * rows + first + tap
            base = start // V7X_SUBLANES * V7X_SUBLANES
            acc = acc + w_ref[tap:tap + 1, :] * buf_ref[start - base, base:base + rows, :]
        mu = jnp.mean(acc, axis=-1, keepdims=True)
        xc = acc - mu
        var = jnp.mean(xc * xc, axis=-1, keepdims=True)
        y = xc * lax.rsqrt(var + EPS) * lg_ref[...] + lb_ref[...]
        o_ref[0, r * rows:(r + 1) * rows, :] = (y * jax.nn.sigmoid(y)).astype(BF16)


def _conv_ln_swish(u, w, b, lg, lb, e, *, tc, rows):
    bsz, s, c = u.shape
    per = tc // CONV_HALO
    vec = pl.BlockSpec((None, 1, c), lambda bi, si: (e, 0, 0))
    blocks = (_nbytes((CONV_HALO, c), F32) + _nbytes((tc, c), F32) + _nbytes((CONV_WIDTH + 1, c), F32)
              + 3 * _nbytes((V7X_SUBLANES, c), F32) + _nbytes((tc, c), BF16))
    buf_shape = (V7X_SUBLANES, tc + CONV_HALO, c)
    return pl.pallas_call(
        functools.partial(_conv_body, tc=tc, rows=rows),
        grid=(bsz, s // tc),
        in_specs=[
            pl.BlockSpec((1, CONV_HALO, c), lambda bi, si: (bi, jnp.maximum(si * per - 1, 0), 0)),
            pl.BlockSpec((1, tc, c), lambda bi, si: (bi, si, 0)),
            pl.BlockSpec((None, CONV_WIDTH, c), lambda bi, si: (e, 0, 0)),
            vec, vec, vec,
        ],
        out_specs=pl.BlockSpec((1, tc, c), lambda bi, si: (bi, si, 0)),
        out_shape=jax.ShapeDtypeStruct((bsz, s, c), BF16),
        scratch_shapes=[pltpu.VMEM(buf_shape, F32)],
        compiler_params=_compiler_params(("parallel", "parallel"), blocks, _nbytes(buf_shape, F32),
                                         2 * _nbytes((tc + CONV_HALO, c), F32)),
        name="conv_ln_swish",
    )(u, u, w, b, lg, lb)


def _head_masks(q):
    lane = lax.broadcasted_iota(jnp.int32, q.shape, 1)
    low = lane < HEAD_DIM
    zero = jnp.zeros_like(q)
    return low, (jnp.where(low, q, zero), jnp.where(low, zero, q))


def _sb_body(q_ref, k_ref, v_ref, tri_ref, o_ref, acc_ref, run_ref, *, tq, tk):
    i = pl.program_id(2)
    low, q_heads = _head_masks(q_ref[0])
    tri = tri_ref[...]
    sub = tri.shape[0]
    t0 = i * tq
    nd = tq // tk
    acc_ref[...] = jnp.zeros_like(acc_ref)
    run_ref[...] = jnp.zeros_like(run_ref)

    def tile(s0, r0, offset):
        kblk = k_ref[0, pl.ds(s0, tk), :]
        vblk = v_ref[0, pl.ds(s0, tk), :]
        rows = tq - r0
        if offset is not None:
            row = lax.broadcasted_iota(jnp.int32, (rows, tk), 0) + r0
            col = lax.broadcasted_iota(jnp.int32, (rows, tk), 1) + offset
            strict = col < row
        for hh in range(HEADS_PER_GROUP):
            z = lax.dot_general(q_heads[hh][r0:, :], kblk, _NT, preferred_element_type=F32)
            sp = _softplus(z)
            if offset is not None:
                sp = jnp.where(strict, sp, 0.0)
            run = run_ref[hh, r0:, :]
            parts = [None] * (tk // V7X_LANES)
            for c in reversed(range(tk // sub)):
                sp_c = sp[:, c * sub:(c + 1) * sub]
                hi, lo = _split2(sp_c)
                after = _dot(hi, tri) + _dot(lo, tri)
                for g in range(sub // V7X_LANES):
                    lanes = slice(c * sub + g * V7X_LANES, c * sub + (g + 1) * V7X_LANES)
                    w = jnp.exp(z[:, lanes] - sp[:, lanes] - after[:, g * V7X_LANES:(g + 1) * V7X_LANES] - run)
                    if offset is not None:
                        w = jnp.where(strict[:, lanes], w, 0.0)
                    parts[lanes.start // V7X_LANES] = w.astype(BF16)
                run = run + jnp.sum(sp_c, axis=1, keepdims=True)
            acc_ref[hh, r0:, :] += _dot(jnp.concatenate(parts, axis=1), vblk)
            run_ref[hh, r0:, :] = run

    for j in range(nd):
        r0 = (nd - 1 - j) * tk
        tile(pl.multiple_of(t0 + r0, tk), r0, r0)

    def strip(n, carry):
        base = t0 - (n + 1) * tq
        for j in range(nd):
            tile(pl.multiple_of(base + (nd - 1 - j) * tk, tk), 0, None)
        return carry

    lax.fori_loop(0, i, strip, 0)
    o_ref[0] = jnp.where(low, acc_ref[0], acc_ref[1]).astype(BF16)


def _strict_lower_ones(n):
    j = lax.broadcasted_iota(jnp.int32, (n, n), 0)
    s = lax.broadcasted_iota(jnp.int32, (n, n), 1)
    return (j > s).astype(BF16)


def _sb_attention(q, k, v, *, tq, tk, sub):
    bsz, s, hd = q.shape
    groups = hd // V7X_LANES
    assert tq % tk == 0 and s % tq == 0 and tk % sub == 0
    qspec = pl.BlockSpec((1, tq, V7X_LANES), lambda b, p, i: (b, i, p))
    kvspec = pl.BlockSpec((1, s, V7X_LANES), lambda b, p, i: (b, 0, p))
    blocks = (2 * _nbytes((tq, V7X_LANES), BF16) + 2 * _nbytes((s, V7X_LANES), BF16) + _nbytes((sub, sub), BF16))
    state = pltpu.VMEM((HEADS_PER_GROUP, tq, V7X_LANES), F32)
    return pl.pallas_call(
        functools.partial(_sb_body, tq=tq, tk=tk),
        grid=(bsz, groups, s // tq),
        in_specs=[qspec, kvspec, kvspec, pl.BlockSpec((sub, sub), lambda b, p, i: (0, 0))],
        out_specs=qspec,
        out_shape=jax.ShapeDtypeStruct((bsz, s, hd), BF16),
        scratch_shapes=[state, state],
        compiler_params=_compiler_params(("parallel", "parallel", "parallel"), blocks,
                                         2 * HEADS_PER_GROUP * _nbytes((tq, V7X_LANES), F32),
                                         12 * _nbytes((tq, tk), F32)),
        name="sb_attention",
    )(q, k, v, _strict_lower_ones(sub))


def _out_proj_body(*refs, widths):
    x_ref = refs[0]
    a_refs = refs[1:1 + len(widths)]
    w_ref = refs[1 + len(widths)]
    o_ref = refs[2 + len(widths)]
    acc = x_ref[...]
    off = 0
    for a_ref, width in zip(a_refs, widths):
        acc = acc + _dot(a_ref[...], w_ref[off:off + width, :])
        off += width
    o_ref[...] = acc


def _out_proj(x, parts, w, e, *, tm):
    n, d = x.shape
    widths = tuple(p.shape[1] for p in parts)
    row = lambda i: (i, 0)
    blocks = (2 * _nbytes((tm, d), F32) + sum(_nbytes((tm, wd), BF16) for wd in widths)
              + _nbytes((sum(widths), d), BF16))
    return pl.pallas_call(
        functools.partial(_out_proj_body, widths=widths),
        grid=(n // tm,),
        in_specs=[pl.BlockSpec((tm, d), row)] + [pl.BlockSpec((tm, wd), row) for wd in widths]
                 + [pl.BlockSpec((None, sum(widths), d), lambda i: (e, 0, 0))],
        out_specs=pl.BlockSpec((tm, d), row),
        out_shape=jax.ShapeDtypeStruct((n, d), F32),
        compiler_params=_compiler_params(("parallel",), blocks, 0, _nbytes((tm, d), F32)),
        name="out_proj",
    )(x, *parts, w)


def _fox_in_body(x_ref, g_ref, w_ref, wft_ref, fb_ref, qg_ref, kg_ref, bd_ref, tri_ref,
                 q_ref, k_ref, v_ref, cum_ref, carry_ref, *, tm, hd):
    s = pl.program_id(1)
    h = _rms_to_bf16(x_ref[0], g_ref[...])
    bd = bd_ref[...]
    chunk = bd.shape[0]
    inv_dim = 1.0 / HEAD_DIM
    for base, gain_ref, out_ref, scale in ((0, qg_ref, q_ref, ATTN_SCALE), (hd, kg_ref, k_ref, 1.0)):
        proj = _dot(h, w_ref[:, base:base + hd])
        for c in range(hd // chunk):
            lanes = slice(c * chunk, (c + 1) * chunk)
            xg = proj[:, lanes]
            hi, lo = _split2(xg * xg)
            ss = _dot(hi, bd) + _dot(lo, bd)
            y = (xg * lax.rsqrt(ss * inv_dim + EPS)) * gain_ref[...]
            out_ref[0, :, lanes] = (y * scale).astype(BF16)
    v_ref[0] = _dot(h, w_ref[:, 2 * hd:3 * hd]).astype(BF16)

    xf = lax.dot_general(wft_ref[...], h, _NT, preferred_element_type=F32) + fb_ref[...]
    logf = jnp.minimum(xf, 0.0) - jnp.log(1.0 + jnp.exp(-jnp.abs(xf)))
    p0 = logf.astype(BF16)
    r0 = logf - p0.astype(F32)
    p1 = r0.astype(BF16)
    p2 = (r0 - p1.astype(F32)).astype(BF16)
    tri = tri_ref[...]
    local = _dot(p0, tri) + _dot(p1, tri) + _dot(p2, tri)

    @pl.when(s == 0)
    def _():
        carry_ref[...] = jnp.zeros_like(carry_ref)

    cum = local + carry_ref[...]
    cum_ref[0] = cum
    carry_ref[...] = cum[:, tm - 1:tm]


NORM_CHUNK = V7X_MXU_DIM


def _head_block_ones():
    a = lax.broadcasted_iota(jnp.int32, (NORM_CHUNK, NORM_CHUNK), 0) // HEAD_DIM
    b = lax.broadcasted_iota(jnp.int32, (NORM_CHUNK, NORM_CHUNK), 1) // HEAD_DIM
    return (a == b).astype(BF16)


def _inclusive_upper_ones(n):
    j = lax.broadcasted_iota(jnp.int32, (n, n), 0)
    s = lax.broadcasted_iota(jnp.int32, (n, n), 1)
    return (j <= s).astype(BF16)


def _fox_in(x, norm, w, wft, fb, qg, kg, layer, o, *, tm):
    bsz, s, d = x.shape
    hd = w.shape[2] // 3
    heads = wft.shape[1]
    tok = lambda b, si: (b, si, 0)
    sel = lambda b, si: (o, 0, 0)
    blocks = (_nbytes((tm, d), F32) + _nbytes((d, 3 * hd), BF16) + _nbytes((heads, d), BF16)
              + 3 * _nbytes((tm, hd), BF16) + _nbytes((heads, tm), F32) + _nbytes((tm, tm), BF16)
              + _nbytes((NORM_CHUNK, NORM_CHUNK), BF16) + 4 * _nbytes((V7X_SUBLANES, d), F32))
    temps = _nbytes((tm, d), BF16) + 2 * _nbytes((tm, hd), F32) + 6 * _nbytes((tm, NORM_CHUNK), F32)
    return pl.pallas_call(
        functools.partial(_fox_in_body, tm=tm, hd=hd),
        grid=(bsz, s // tm),
        in_specs=[
            pl.BlockSpec((1, tm, d), tok),
            pl.BlockSpec((None, 1, d), lambda b, si: (layer, 0, 0)),
            pl.BlockSpec((None, d, 3 * hd), sel),
            pl.BlockSpec((None, heads, d), sel),
            pl.BlockSpec((None, heads, 1), sel),
            pl.BlockSpec((None, 1, NORM_CHUNK), sel),
            pl.BlockSpec((None, 1, NORM_CHUNK), sel),
            pl.BlockSpec((NORM_CHUNK, NORM_CHUNK), lambda b, si: (0, 0)),
            pl.BlockSpec((tm, tm), lambda b, si: (0, 0)),
        ],
        out_specs=[pl.BlockSpec((1, tm, hd), tok)] * 3 + [pl.BlockSpec((1, heads, tm), lambda b, si: (b, 0, si))],
        out_shape=[jax.ShapeDtypeStruct((bsz, s, hd), BF16)] * 3 + [jax.ShapeDtypeStruct((bsz, heads, s), F32)],
        scratch_shapes=[pltpu.VMEM((heads, 1), F32)],
        compiler_params=_compiler_params(("parallel", "arbitrary"), blocks,
                                         _nbytes((heads, V7X_LANES), F32), temps),
        name="fox_in",
    )(x, norm, w, wft, fb, qg, kg, _head_block_ones(), _inclusive_upper_ones(tm))


def _fox_body(q_ref, k_ref, v_ref, cum_ref, o_ref, vx_ref, acc_ref, m_ref, *, tq, tk):
    p = pl.program_id(1)
    i = pl.program_id(2)
    low, q_heads = _head_masks(q_ref[0])
    t0 = i * tq
    nd = tq // tk

    @pl.when(i == 0)
    def _():
        v = v_ref[0]
        lane = lax.broadcasted_iota(jnp.int32, v.shape, 1)
        ones = jnp.ones_like(v)
        vx_ref[0] = jnp.where(lane < HEAD_DIM, v, ones)
        vx_ref[1] = jnp.where(lane < HEAD_DIM, ones, v)

    acc_ref[...] = jnp.zeros_like(acc_ref)
    m_ref[...] = jnp.full_like(m_ref, -jnp.inf)

    def tile(s0, width, r0, offset):
        kblk = k_ref[0, pl.ds(s0, width), :]
        rows = tq - r0
        if offset is not None:
            row = lax.broadcasted_iota(jnp.int32, (rows, width), 0) + r0
            col = lax.broadcasted_iota(jnp.int32, (rows, width), 1) + offset
            causal = col <= row
        for hh in range(HEADS_PER_GROUP):
            decay = cum_ref[0, pl.ds(p * HEADS_PER_GROUP + hh, 1), pl.ds(s0, width)]
            z = lax.dot_general(q_heads[hh][r0:, :], kblk, _NT, preferred_element_type=F32) - decay
            if offset is not None:
                z = jnp.where(causal, z, -jnp.inf)
            m_old = m_ref[hh, r0:, :]
            m_new = jnp.maximum(m_old, jnp.max(z, axis=1, keepdims=True))
            alpha = jnp.exp(m_old - m_new)
            pr = jnp.concatenate(
                [jnp.exp(z[:, g * V7X_LANES:(g + 1) * V7X_LANES] - m_new) for g in range(width // V7X_LANES)], axis=1)
            acc_ref[hh, r0:, :] = alpha * acc_ref[hh, r0:, :] + _dot(pr.astype(BF16), vx_ref[hh, pl.ds(s0, width), :])
            m_ref[hh, r0:, :] = m_new

    for j in range(nd):
        r0 = (nd - 1 - j) * tk
        tile(pl.multiple_of(t0 + r0, tk), tk, r0, r0)

    def strip(n, carry):
        tile(pl.multiple_of(t0 - (n + 1) * tq, tq), tq, 0, None)
        return carry

    lax.fori_loop(0, i, strip, 0)
    outs = [acc_ref[hh] / pltpu.roll(acc_ref[hh], HEAD_DIM, axis=1) for hh in range(HEADS_PER_GROUP)]
    o_ref[0] = jnp.where(low, outs[0], outs[1]).astype(BF16)


def _fox_attention(q, k, v, cum, *, tq, tk):
    bsz, s, hd = q.shape
    heads = cum.shape[1]
    groups = hd // V7X_LANES
    assert tq % tk == 0 and s % tq == 0
    qspec = pl.BlockSpec((1, tq, V7X_LANES), lambda b, p, i: (b, i, p))
    kvspec = pl.BlockSpec((1, s, V7X_LANES), lambda b, p, i: (b, 0, p))
    blocks = (2 * _nbytes((tq, V7X_LANES), BF16) + 2 * _nbytes((s, V7X_LANES), BF16) + _nbytes((heads, s), F32))
    scratch = [pltpu.VMEM((HEADS_PER_GROUP, s, V7X_LANES), BF16),
               pltpu.VMEM((HEADS_PER_GROUP, tq, V7X_LANES), F32), pltpu.VMEM((HEADS_PER_GROUP, tq, V7X_LANES), F32)]
    return pl.pallas_call(
        functools.partial(_fox_body, tq=tq, tk=tk),
        grid=(bsz, groups, s // tq),
        in_specs=[qspec, kvspec, kvspec, pl.BlockSpec((1, heads, s), lambda b, p, i: (b, 0, 0))],
        out_specs=qspec,
        out_shape=jax.ShapeDtypeStruct((bsz, s, hd), BF16),
        scratch_shapes=scratch,
        compiler_params=_compiler_params(("parallel", "parallel", "arbitrary"), blocks,
                                         HEADS_PER_GROUP * (_nbytes((s, V7X_LANES), BF16)
                                                            + 2 * _nbytes((tq, V7X_LANES), F32)),
                                         12 * _nbytes((tq, tq), F32)),
        name="fox_attention",
    )(q, k, v, cum)


FFN_ROWS = 512
CONV_ROWS = 256
CONV_ACC_ROWS = 32
ATTN_Q = 2048
ATTN_K = 512
SB_SUFFIX = V7X_MXU_DIM


def kernel(x, ffn1_norm, ffn1_w_in, ffn1_w_out, mix_norm, ffn2_norm, ffn2_w_in, ffn2_w_out, ab_w_in, conv_w, conv_b,
           conv_ln_g, conv_ln_b, ab_w_out, fox_w_in, fox_f_bias, fox_q_norm, fox_k_norm, fox_w_out):
    bsz, s, d = x.shape
    depth = ffn1_norm.shape[0]
    n = bsz * s
    row3 = lambda a: a.reshape(a.shape[0], 1, a.shape[1])

    ffn1_norm, ffn2_norm, mix_norm = row3(ffn1_norm), row3(ffn2_norm), row3(mix_norm)
    conv_b, conv_ln_g, conv_ln_b = row3(conv_b), row3(conv_ln_g), row3(conv_ln_b)
    ffn1_w_in, ffn1_w_out = ffn1_w_in.astype(BF16), ffn1_w_out.astype(BF16)
    ffn2_w_in, ffn2_w_out = ffn2_w_in.astype(BF16), ffn2_w_out.astype(BF16)
    ab_w_in, ab_w_out = ab_w_in.astype(BF16), ab_w_out.astype(BF16)
    fox_w_out = fox_w_out.astype(BF16)
    hd = fox_w_out.shape[1]
    fox_w_qkv = fox_w_in[:, :, :3 * hd].astype(BF16)
    fox_w_ft = jnp.swapaxes(fox_w_in[:, :, 3 * hd:], 1, 2).astype(BF16)
    fox_fb = fox_f_bias[:, :, None]
    fox_qg = row3(jnp.tile(fox_q_norm, (1, NORM_CHUNK // HEAD_DIM)))
    fox_kg = row3(jnp.tile(fox_k_norm, (1, NORM_CHUNK // HEAD_DIM)))

    xf = x.reshape(n, d)
    for layer in range(depth):
        xf = _ffn(xf, ffn1_norm, ffn1_w_in, ffn1_w_out, layer, tm=FFN_ROWS)
        if layer % 2 == 0:
            e = layer // 2
            u, q, k, v = _ab_in(xf, mix_norm, ab_w_in, layer, e, tm=FFN_ROWS)
            c = u.shape[1]
            u = _conv_ln_swish(u.reshape(bsz, s, c), conv_w, conv_b, conv_ln_g, conv_ln_b, e,
                               tc=CONV_ROWS, rows=CONV_ACC_ROWS)
            att = _sb_attention(q.reshape(bsz, s, c), k.reshape(bsz, s, c), v.reshape(bsz, s, c),
                                tq=min(ATTN_Q, s), tk=ATTN_K, sub=SB_SUFFIX)
            xf = _out_proj(xf, [u.reshape(n, c), att.reshape(n, c)], ab_w_out, e, tm=FFN_ROWS)
        else:
            o = layer // 2
            q, k, v, cum = _fox_in(xf.reshape(bsz, s, d), mix_norm, fox_w_qkv, fox_w_ft, fox_fb, fox_qg, fox_kg,
                                   layer, o, tm=FFN_ROWS)
            att = _fox_attention(q, k, v, cum, tq=min(ATTN_Q, s), tk=ATTN_K)
            xf = _out_proj(xf, [att.reshape(n, hd)], fox_w_out, o, tm=FFN_ROWS)
        xf = _ffn(xf, ffn2_norm, ffn2_w_in, ffn2_w_out, layer, tm=FFN_ROWS)
    return xf.reshape(bsz, s, d)
```

```python
import functools

import jax
import jax.numpy as jnp
from jax import lax
from jax.experimental import pallas as pl
from jax.experimental.pallas import tpu as pltpu

F32 = jnp.float32
BF16 = jnp.bfloat16

EPS = 1e-6
HEAD_DIM = 64
CONV_WIDTH = 31
ATTN_SCALE = HEAD_DIM ** -0.5

V7X_LANES = 128
V7X_SUBLANES = 8
V7X_MXU_DIM = 256
V7X_VMEM_BYTES = 64 * 1024 * 1024

HEADS_PER_GROUP = V7X_LANES // HEAD_DIM
CONV_HALO = 32

_NT = (((1,), (1,)), ((), ()))


def _nbytes(shape, dtype):
    n = 1
    for d in shape:
        n *= d
    return n * jnp.dtype(dtype).itemsize


def _compiler_params(semantics, block_bytes, scratch_bytes, temp_bytes):
    need = 2 * block_bytes + scratch_bytes + temp_bytes
    mib = 1024 * 1024
    limit = min(-(-need // mib) * mib, V7X_VMEM_BYTES - 4 * mib)
    return pltpu.CompilerParams(dimension_semantics=semantics, vmem_limit_bytes=limit)


def _dot(a, b):
    return jnp.dot(a, b, preferred_element_type=F32)


def _rms_to_bf16(x, g):
    ms = jnp.mean(x * x, axis=-1, keepdims=True)
    return ((x * lax.rsqrt(ms + EPS)) * g).astype(BF16)


def _split2(x):
    hi = x.astype(BF16)
    lo = (x - hi.astype(F32)).astype(BF16)
    return hi, lo


def _softplus(z):
    return jnp.maximum(z, 0.0) + jnp.log(1.0 + jnp.exp(-jnp.abs(z)))


def _ffn_body(*refs, widths):
    x_ref = refs[0]
    part_refs = refs[1:1 + len(widths)]
    g_ref, wg_ref, wu_ref, wo_ref = refs[1 + len(widths):5 + len(widths)]
    o_ref = refs[-1]
    x = x_ref[...]
    if widths:
        wm_ref = refs[5 + len(widths)]
        off = 0
        for part_ref, width in zip(part_refs, widths):
            x = x + _dot(part_ref[...], wm_ref[off:off + width, :])
            off += width
    h = _rms_to_bf16(x, g_ref[...])
    gate = _dot(h, wg_ref[...])
    up = _dot(h, wu_ref[...])
    act = (gate * jax.nn.sigmoid(gate) * up).astype(BF16)
    o_ref[...] = x + 0.5 * _dot(act, wo_ref[...])


def _resident(block_shape, index_map):
    return pl.BlockSpec(block_shape, index_map, pipeline_mode=pl.Buffered(1))


def _ffn(x, norm, w_in, w_out, layer, *, tm, parts=(), w_mix=None, mix_index=0):
    n, d = x.shape
    d_ff = w_out.shape[1]
    widths = tuple(p.shape[1] for p in parts)
    row = lambda i: (i, 0)
    blocks = _nbytes((tm, d), F32) * 2 + _nbytes((1, d), F32) + sum(_nbytes((tm, wd), BF16) for wd in widths)
    weights = _nbytes((d, d_ff), BF16) * 3 + _nbytes((sum(widths), d), BF16)
    temps = 2 * _nbytes((tm, d_ff), F32) + _nbytes((tm, d_ff), BF16) + 2 * _nbytes((tm, d), F32)
    in_specs = [pl.BlockSpec((tm, d), row)] + [pl.BlockSpec((tm, wd), row) for wd in widths] + [
        pl.BlockSpec((None, 1, d), lambda i: (layer, 0, 0)),
        _resident((None, d, d_ff), lambda i: (layer, 0, 0)),
        _resident((None, d, d_ff), lambda i: (layer, 0, 1)),
        _resident((None, d_ff, d), lambda i: (layer, 0, 0)),
    ]
    args = [x, *parts, norm, w_in, w_in, w_out]
    if widths:
        in_specs.append(_resident((None, sum(widths), d), lambda i: (mix_index, 0, 0)))
        args.append(w_mix)
    return pl.pallas_call(
        functools.partial(_ffn_body, widths=widths),
        grid=(n // tm,),
        in_specs=in_specs,
        out_specs=pl.BlockSpec((tm, d), row),
        out_shape=jax.ShapeDtypeStruct((n, d), F32),
        compiler_params=_compiler_params(("parallel",), blocks, weights, temps),
        name="mix_ffn" if widths else "ffn",
    )(*args)


def _ab_in_body(x_ref, g_ref, w_ref, cw_ref, cb_ref, lg_ref, lb_ref, u_ref, q_ref, k_ref, v_ref,
                buf_ref, tail_ref, *, tm, c, rows):
    @pl.when(pl.program_id(1) == 0)
    def _():
        tail_ref[...] = jnp.zeros_like(tail_ref)

    h = _rms_to_bf16(x_ref[0], g_ref[...])
    glu = _dot(h, w_ref[:, 0:c]) * jax.nn.sigmoid(_dot(h, w_ref[:, c:2 * c]))
    q_ref[0] = (_dot(h, w_ref[:, 2 * c:3 * c]) * ATTN_SCALE).astype(BF16)
    k_ref[0] = _dot(h, w_ref[:, 3 * c:4 * c]).astype(BF16)
    v_ref[0] = _dot(h, w_ref[:, 4 * c:5 * c]).astype(BF16)

    buf_ref[0, 0:CONV_HALO, :] = tail_ref[...]
    buf_ref[0, CONV_HALO:, :] = glu
    tail_ref[...] = glu[tm - CONV_HALO:, :]
    moved = tm + CONV_HALO - V7X_SUBLANES
    for p in range(1, V7X_SUBLANES):
        buf_ref[p, 0:moved, :] = buf_ref[0, p:p + moved, :]
    first = CONV_HALO - (CONV_WIDTH - 1)
    for r in range(tm // rows):
        acc = jnp.broadcast_to(cb_ref[...], (rows, c))
        for tap in range(CONV_WIDTH):
            start = r * rows + first + tap
            base = start // V7X_SUBLANES * V7X_SUBLANES
            acc = acc + cw_ref[tap:tap + 1, :] * buf_ref[start - base, base:base + rows, :]
        mu = jnp.mean(acc, axis=-1, keepdims=True)
        xc = acc - mu
        var = jnp.mean(xc * xc, axis=-1, keepdims=True)
        y = xc * lax.rsqrt(var + EPS) * lg_ref[...] + lb_ref[...]
        u_ref[0, r * rows:(r + 1) * rows, :] = (y * jax.nn.sigmoid(y)).astype(BF16)


def _ab_in(x, norm, w, conv_w, conv_b, ln_g, ln_b, layer, e, *, tm, rows):
    bsz, s, d = x.shape
    width = w.shape[2]
    c = width // 5
    tok = lambda b, si: (b, si, 0)
    sel = lambda b, si: (e, 0, 0)
    vec = pl.BlockSpec((None, 1, c), sel)
    blocks = (_nbytes((tm, d), F32) + _nbytes((1, d), F32) + 4 * _nbytes((tm, c), BF16)
              + _nbytes((CONV_WIDTH + 1, c), F32) + 3 * _nbytes((V7X_SUBLANES, c), F32))
    buf_shape = (V7X_SUBLANES, tm + CONV_HALO, c)
    scratch = _nbytes(buf_shape, F32) + _nbytes((CONV_HALO, c), F32) + _nbytes((d, width), BF16)
    temps = 4 * _nbytes((tm, c), F32) + _nbytes((tm, d), BF16)
    return pl.pallas_call(
        functools.partial(_ab_in_body, tm=tm, c=c, rows=rows),
        grid=(bsz, s // tm),
        in_specs=[
            pl.BlockSpec((1, tm, d), tok),
            pl.BlockSpec((None, 1, d), lambda b, si: (layer, 0, 0)),
            _resident((None, d, width), sel),
            pl.BlockSpec((None, CONV_WIDTH, c), sel),
            vec, vec, vec,
        ],
        out_specs=[pl.BlockSpec((1, tm, c), tok)] * 4,
        out_shape=[jax.ShapeDtypeStruct((bsz, s, c), BF16)] * 4,
        scratch_shapes=[pltpu.VMEM(buf_shape, F32), pltpu.VMEM((CONV_HALO, c), F32)],
        compiler_params=_compiler_params(("parallel", "arbitrary"), blocks, scratch, temps),
        name="ab_in",
    )(x, norm, w, conv_w, conv_b, ln_g, ln_b)


SB_DEAD_MASS = 110.0


def _sb_body(q_ref, k_ref, v_ref, tri_ref, o_ref, acc_ref, run_ref, *, seq, chunk):
    tri = tri_ref[...]
    sub = tri.shape[0]

    def tile(r0, rows, s0, keys, diagonal, first):
        q = q_ref[0, r0:r0 + rows, :]
        zero = jnp.zeros_like(q)
        low = lax.broadcasted_iota(jnp.int32, q.shape, 1) < HEAD_DIM
        kblk = k_ref[0, s0:s0 + keys, :]
        vblk = v_ref[0, s0:s0 + keys, :]
        if diagonal:
            row = lax.broadcasted_iota(jnp.int32, (rows, keys), 0) + r0
            col = lax.broadcasted_iota(jnp.int32, (rows, keys), 1) + s0
            strict = col < row
        for hh in range(HEADS_PER_GROUP):
            q_head = jnp.where(low, q, zero) if hh == 0 else jnp.where(low, zero, q)
            z = lax.dot_general(q_head, kblk, _NT, preferred_element_type=F32)
            sp = _softplus(z)
            if diagonal:
                sp = jnp.where(strict, sp, 0.0)
            run = jnp.zeros((rows, V7X_LANES), F32) if first else run_ref[hh, r0:r0 + rows, :]
            parts = [None] * (keys // V7X_LANES)
            for c in reversed(range(keys // sub)):
                sp_c = sp[:, c * sub:(c + 1) * sub]
                hi, lo = _split2(sp_c)
                after = _dot(hi, tri) + _dot(lo, tri)
                for g in range(sub // V7X_LANES):
                    lanes = slice(c * sub + g * V7X_LANES, c * sub + (g + 1) * V7X_LANES)
                    w = jnp.exp(z[:, lanes] - sp[:, lanes] - after[:, g * V7X_LANES:(g + 1) * V7X_LANES] - run)
                    if diagonal:
                        w = jnp.where(strict[:, lanes], w, 0.0)
                    parts[lanes.start // V7X_LANES] = w.astype(BF16)
                run = run + jnp.sum(sp_c, axis=1, keepdims=True)
            out = _dot(jnp.concatenate(parts, axis=1), vblk)
            if first:
                acc_ref[hh, r0:r0 + rows, :] = out
            else:
                acc_ref[hh, r0:r0 + rows, :] += out
            run_ref[hh, r0:r0 + rows, :] = run

    n_chunks = seq // chunk
    tile(0, chunk, 0, chunk, True, True)
    for c in range(1, n_chunks):
        tile(c * chunk, chunk, (c - 1) * chunk, 2 * chunk, True, True)

    if n_chunks > 2:
        @pl.when(jnp.min(run_ref[:, 2 * chunk:, :]) < SB_DEAD_MASS)
        def _():
            for kb in reversed(range(n_chunks - 2)):
                r0 = (kb + 2) * chunk
                tile(r0, seq - r0, kb * chunk, chunk, False, False)

    o_ref[0] = jnp.where(lax.broadcasted_iota(jnp.int32, (seq, V7X_LANES), 1) < HEAD_DIM,
                         acc_ref[0], acc_ref[1]).astype(BF16)


def _strict_lower_ones(n):
    j = lax.broadcasted_iota(jnp.int32, (n, n), 0)
    s = lax.broadcasted_iota(jnp.int32, (n, n), 1)
    return (j > s).astype(BF16)


def _sb_attention(q, k, v, *, chunk):
    bsz, s, hd = q.shape
    groups = hd // V7X_LANES
    assert s % chunk == 0
    spec = pl.BlockSpec((1, s, V7X_LANES), lambda b, p: (b, 0, p))
    blocks = 4 * _nbytes((s, V7X_LANES), BF16) + _nbytes((chunk, chunk), BF16)
    state = pltpu.VMEM((HEADS_PER_GROUP, s, V7X_LANES), F32)
    return pl.pallas_call(
        functools.partial(_sb_body, seq=s, chunk=chunk),
        grid=(bsz, groups),
        in_specs=[spec, spec, spec, pl.BlockSpec((chunk, chunk), lambda b, p: (0, 0))],
        out_specs=spec,
        out_shape=jax.ShapeDtypeStruct((bsz, s, hd), BF16),
        scratch_shapes=[state, state],
        compiler_params=_compiler_params(("parallel", "parallel"), blocks,
                                         2 * HEADS_PER_GROUP * _nbytes((s, V7X_LANES), F32),
                                         12 * _nbytes((s, chunk), F32)),
        name="sb_attention",
    )(q, k, v, _strict_lower_ones(chunk))


def _fox_in_body(x_ref, g_ref, w_ref, wft_ref, fb_ref, qg_ref, kg_ref, bd_ref, tri_ref,
                 q_ref, k_ref, v_ref, cum_ref, carry_ref, *, tm, hd):
    s = pl.program_id(1)
    h = _rms_to_bf16(x_ref[0], g_ref[...])
    bd = bd_ref[...]
    chunk = bd.shape[0]
    inv_dim = 1.0 / HEAD_DIM
    for base, gain_ref, out_ref, scale in ((0, qg_ref, q_ref, ATTN_SCALE), (hd, kg_ref, k_ref, 1.0)):
        proj = _dot(h, w_ref[:, base:base + hd])
        for c in range(hd // chunk):
            lanes = slice(c * chunk, (c + 1) * chunk)
            xg = proj[:, lanes]
            hi, lo = _split2(xg * xg)
            ss = _dot(hi, bd) + _dot(lo, bd)
            y = (xg * lax.rsqrt(ss * inv_dim + EPS)) * gain_ref[...]
            out_ref[0, :, lanes] = (y * scale).astype(BF16)
    v_ref[0] = _dot(h, w_ref[:, 2 * hd:3 * hd]).astype(BF16)

    xf = lax.dot_general(wft_ref[...], h, _NT, preferred_element_type=F32) + fb_ref[...]
    logf = jnp.minimum(xf, 0.0) - jnp.log(1.0 + jnp.exp(-jnp.abs(xf)))
    p0 = logf.astype(BF16)
    r0 = logf - p0.astype(F32)
    p1 = r0.astype(BF16)
    p2 = (r0 - p1.astype(F32)).astype(BF16)
    tri = tri_ref[...]
    local = _dot(p0, tri) + _dot(p1, tri) + _dot(p2, tri)

    @pl.when(s == 0)
    def _():
        carry_ref[...] = jnp.zeros_like(carry_ref)

    cum = local + carry_ref[...]
    cum_ref[0] = cum
    carry_ref[...] = cum[:, tm - 1:tm]


NORM_CHUNK = V7X_MXU_DIM


def _head_block_ones():
    a = lax.broadcasted_iota(jnp.int32, (NORM_CHUNK, NORM_CHUNK), 0) // HEAD_DIM
    b = lax.broadcasted_iota(jnp.int32, (NORM_CHUNK, NORM_CHUNK), 1) // HEAD_DIM
    return (a == b).astype(BF16)


def _inclusive_upper_ones(n):
    j = lax.broadcasted_iota(jnp.int32, (n, n), 0)
    s = lax.broadcasted_iota(jnp.int32, (n, n), 1)
    return (j <= s).astype(BF16)


def _fox_in(x, norm, w, wft, fb, qg, kg, layer, o, *, tm):
    bsz, s, d = x.shape
    hd = w.shape[2] // 3
    heads = wft.shape[1]
    tok = lambda b, si: (b, si, 0)
    sel = lambda b, si: (o, 0, 0)
    blocks = (_nbytes((tm, d), F32) + _nbytes((d, 3 * hd), BF16) + _nbytes((heads, d), BF16)
              + 3 * _nbytes((tm, hd), BF16) + _nbytes((heads, tm), F32) + _nbytes((tm, tm), BF16)
              + _nbytes((NORM_CHUNK, NORM_CHUNK), BF16) + 4 * _nbytes((V7X_SUBLANES, d), F32))
    temps = _nbytes((tm, d), BF16) + 2 * _nbytes((tm, hd), F32) + 6 * _nbytes((tm, NORM_CHUNK), F32)
    return pl.pallas_call(
        functools.partial(_fox_in_body, tm=tm, hd=hd),
        grid=(bsz, s // tm),
        in_specs=[
            pl.BlockSpec((1, tm, d), tok),
            pl.BlockSpec((None, 1, d), lambda b, si: (layer, 0, 0)),
            pl.BlockSpec((None, d, 3 * hd), sel),
            pl.BlockSpec((None, heads, d), sel),
            pl.BlockSpec((None, heads, 1), sel),
            pl.BlockSpec((None, 1, NORM_CHUNK), sel),
            pl.BlockSpec((None, 1, NORM_CHUNK), sel),
            pl.BlockSpec((NORM_CHUNK, NORM_CHUNK), lambda b, si: (0, 0)),
            pl.BlockSpec((tm, tm), lambda b, si: (0, 0)),
        ],
        out_specs=[pl.BlockSpec((1, tm, hd), tok)] * 3 + [pl.BlockSpec((1, heads, tm), lambda b, si: (b, 0, si))],
        out_shape=[jax.ShapeDtypeStruct((bsz, s, hd), BF16)] * 3 + [jax.ShapeDtypeStruct((bsz, heads, s), F32)],
        scratch_shapes=[pltpu.VMEM((heads, 1), F32)],
        compiler_params=_compiler_params(("parallel", "arbitrary"), blocks,
                                         _nbytes((heads, V7X_LANES), F32), temps),
        name="fox_in",
    )(x, norm, w, wft, fb, qg, kg, _head_block_ones(), _inclusive_upper_ones(tm))


def _fox_body(q_ref, k_ref, v_ref, cum_ref, o_ref, vx_ref, acc_ref, m_ref, *, seq, tk):
    p = pl.program_id(1)
    v = v_ref[0]
    low = lax.broadcasted_iota(jnp.int32, v.shape, 1) < HEAD_DIM
    ones = jnp.ones_like(v)
    vx_ref[0] = jnp.where(low, v, ones)
    vx_ref[1] = jnp.where(low, ones, v)
    acc_ref[...] = jnp.zeros_like(acc_ref)
    m_ref[...] = jnp.full_like(m_ref, -jnp.inf)

    def tile(r0, rows, s0, keys, diagonal):
        q = q_ref[0, r0:r0 + rows, :]
        zero = jnp.zeros_like(q)
        low_q = lax.broadcasted_iota(jnp.int32, q.shape, 1) < HEAD_DIM
        kblk = k_ref[0, s0:s0 + keys, :]
        if diagonal:
            row = lax.broadcasted_iota(jnp.int32, (rows, keys), 0) + r0
            col = lax.broadcasted_iota(jnp.int32, (rows, keys), 1) + s0
            causal = col <= row
        for hh in range(HEADS_PER_GROUP):
            q_head = jnp.where(low_q, q, zero) if hh == 0 else jnp.where(low_q, zero, q)
            decay = cum_ref[0, pl.ds(p * HEADS_PER_GROUP + hh, 1), s0:s0 + keys]
            z = lax.dot_general(q_head, kblk, _NT, preferred_element_type=F32) - decay
            if diagonal:
                z = jnp.where(causal, z, -jnp.inf)
            m_old = m_ref[hh, r0:r0 + rows, :]
            m_new = jnp.maximum(m_old, jnp.max(z, axis=1, keepdims=True))
            alpha = jnp.exp(m_old - m_new)
            pr = jnp.concatenate(
                [jnp.exp(z[:, g * V7X_LANES:(g + 1) * V7X_LANES] - m_new) for g in range(keys // V7X_LANES)], axis=1)
            acc_ref[hh, r0:r0 + rows, :] = (alpha * acc_ref[hh, r0:r0 + rows, :]
                                            + _dot(pr.astype(BF16), vx_ref[hh, s0:s0 + keys, :]))
            m_ref[hh, r0:r0 + rows, :] = m_new

    for base in reversed(range(0, seq, tk)):
        tile(base, seq - base, base, tk, True)

    outs = [acc_ref[hh] / pltpu.roll(acc_ref[hh], HEAD_DIM, axis=1) for hh in range(HEADS_PER_GROUP)]
    o_ref[0] = jnp.where(low, outs[0], outs[1]).astype(BF16)


def _fox_attention(q, k, v, cum, *, tk):
    bsz, s, hd = q.shape
    heads = cum.shape[1]
    groups = hd // V7X_LANES
    assert s % tk == 0
    spec = pl.BlockSpec((1, s, V7X_LANES), lambda b, p: (b, 0, p))
    blocks = 4 * _nbytes((s, V7X_LANES), BF16) + _nbytes((heads, s), F32)
    state = pltpu.VMEM((HEADS_PER_GROUP, s, V7X_LANES), F32)
    return pl.pallas_call(
        functools.partial(_fox_body, seq=s, tk=tk),
        grid=(bsz, groups),
        in_specs=[spec, spec, spec, pl.BlockSpec((1, heads, s), lambda b, p: (b, 0, 0))],
        out_specs=spec,
        out_shape=jax.ShapeDtypeStruct((bsz, s, hd), BF16),
        scratch_shapes=[pltpu.VMEM((HEADS_PER_GROUP, s, V7X_LANES), BF16), state, state],
        compiler_params=_compiler_params(("parallel", "parallel"), blocks,
                                         HEADS_PER_GROUP * (_nbytes((s, V7X_LANES), BF16)
                                                            + 2 * _nbytes((s, V7X_LANES), F32)),
                                         12 * _nbytes((s, tk), F32)),
        name="fox_attention",
    )(q, k, v, cum)


FFN_ROWS = 512
CONV_ACC_ROWS = 64
FOX_KEYS = 512
SB_CHUNK = V7X_MXU_DIM


def kernel(x, ffn1_norm, ffn1_w_in, ffn1_w_out, mix_norm, ffn2_norm, ffn2_w_in, ffn2_w_out, ab_w_in, conv_w, conv_b,
           conv_ln_g, conv_ln_b, ab_w_out, fox_w_in, fox_f_bias, fox_q_norm, fox_k_norm, fox_w_out):
    bsz, s, d = x.shape
    depth = ffn1_norm.shape[0]
    n = bsz * s
    row3 = lambda a: a.reshape(a.shape[0], 1, a.shape[1])

    ffn1_norm, ffn2_norm, mix_norm = row3(ffn1_norm), row3(ffn2_norm), row3(mix_norm)
    conv_b, conv_ln_g, conv_ln_b = row3(conv_b), row3(conv_ln_g), row3(conv_ln_b)
    ffn1_w_in, ffn1_w_out = ffn1_w_in.astype(BF16), ffn1_w_out.astype(BF16)
    ffn2_w_in, ffn2_w_out = ffn2_w_in.astype(BF16), ffn2_w_out.astype(BF16)
    ab_w_in, ab_w_out = ab_w_in.astype(BF16), ab_w_out.astype(BF16)
    fox_w_out = fox_w_out.astype(BF16)
    hd = fox_w_out.shape[1]
    fox_w_qkv = fox_w_in[:, :, :3 * hd].astype(BF16)
    fox_w_ft = jnp.swapaxes(fox_w_in[:, :, 3 * hd:], 1, 2).astype(BF16)
    fox_fb = fox_f_bias[:, :, None]
    fox_qg = row3(jnp.tile(fox_q_norm, (1, NORM_CHUNK // HEAD_DIM)))
    fox_kg = row3(jnp.tile(fox_k_norm, (1, NORM_CHUNK // HEAD_DIM)))

    xf = x.reshape(n, d)
    for layer in range(depth):
        xf = _ffn(xf, ffn1_norm, ffn1_w_in, ffn1_w_out, layer, tm=FFN_ROWS)
        if layer % 2 == 0:
            e = layer // 2
            u, q, k, v = _ab_in(xf.reshape(bsz, s, d), mix_norm, ab_w_in, conv_w, conv_b, conv_ln_g, conv_ln_b,
                                layer, e, tm=FFN_ROWS, rows=CONV_ACC_ROWS)
            att = _sb_attention(q, k, v, chunk=SB_CHUNK)
            c = u.shape[2]
            parts, w_mix, mix_index = [u.reshape(n, c), att.reshape(n, c)], ab_w_out, e
        else:
            o = layer // 2
            q, k, v, cum = _fox_in(xf.reshape(bsz, s, d), mix_norm, fox_w_qkv, fox_w_ft, fox_fb, fox_qg, fox_kg,
                                   layer, o, tm=FFN_ROWS)
            att = _fox_attention(q, k, v, cum, tk=FOX_KEYS)
            parts, w_mix, mix_index = [att.reshape(n, hd)], fox_w_out, o
        xf = _ffn(xf, ffn2_norm, ffn2_w_in, ffn2_w_out, layer, tm=FFN_ROWS,
                  parts=parts, w_mix=w_mix, mix_index=mix_index)
    return xf.reshape(bsz, s, d)
```

```python
import functools
import math

import jax
import jax.numpy as jnp
from jax import lax
from jax.experimental import pallas as pl
from jax.experimental.pallas import tpu as pltpu

F32 = jnp.float32
BF16 = jnp.bfloat16

EPS = 1e-6
HEAD_DIM = 64
CONV_WIDTH = 31
ATTN_SCALE = HEAD_DIM ** -0.5
LOG2_E = math.log2(math.e)

V7X_LANES = 128
V7X_SUBLANES = 8
V7X_MXU_DIM = 256
V7X_VMEM_BYTES = 64 * 1024 * 1024

HEADS_PER_GROUP = V7X_LANES // HEAD_DIM
CONV_HALO = 32

_NT = (((1,), (1,)), ((), ()))


def _nbytes(shape, dtype):
    n = 1
    for d in shape:
        n *= d
    return n * jnp.dtype(dtype).itemsize


def _compiler_params(semantics, block_bytes, scratch_bytes, temp_bytes):
    need = 2 * block_bytes + scratch_bytes + temp_bytes
    mib = 1024 * 1024
    limit = min(-(-need // mib) * mib, V7X_VMEM_BYTES - 4 * mib)
    return pltpu.CompilerParams(dimension_semantics=semantics, vmem_limit_bytes=limit)


def _dot(a, b):
    return jnp.dot(a, b, preferred_element_type=F32)


def _rms_to_bf16(x, g):
    ms = jnp.mean(x * x, axis=-1, keepdims=True)
    return ((x * lax.rsqrt(ms + EPS)) * g).astype(BF16)


def _softplus(z):
    return jnp.maximum(z, 0.0) + jnp.log(1.0 + jnp.exp2(jnp.abs(z) * -LOG2_E))


def _ffn_body(*refs, widths):
    x_ref = refs[0]
    part_refs = refs[1:1 + len(widths)]
    g_ref, wg_ref, wu_ref, wo_ref = refs[1 + len(widths):5 + len(widths)]
    o_ref = refs[-1]
    x = x_ref[...]
    if widths:
        wm_ref = refs[5 + len(widths)]
        off = 0
        for part_ref, width in zip(part_refs, widths):
            x = x + _dot(part_ref[...], wm_ref[off:off + width, :])
            off += width
    h = _rms_to_bf16(x, g_ref[...])
    gate = _dot(h, wg_ref[...])
    up = _dot(h, wu_ref[...])
    act = (gate * jax.nn.sigmoid(gate) * up).astype(BF16)
    o_ref[...] = x + 0.5 * _dot(act, wo_ref[...])


def _resident(block_shape, index_map):
    return pl.BlockSpec(block_shape, index_map, pipeline_mode=pl.Buffered(1))


def _ffn(x, norm, w_in, w_out, layer, *, tm, parts=(), w_mix=None, mix_index=0):
    n, d = x.shape
    d_ff = w_out.shape[1]
    widths = tuple(p.shape[1] for p in parts)
    row = lambda i: (i, 0)
    blocks = _nbytes((tm, d), F32) * 2 + _nbytes((1, d), F32) + sum(_nbytes((tm, wd), BF16) for wd in widths)
    weights = _nbytes((d, d_ff), BF16) * 3 + _nbytes((sum(widths), d), BF16)
    temps = 2 * _nbytes((tm, d_ff), F32) + _nbytes((tm, d_ff), BF16) + 2 * _nbytes((tm, d), F32)
    in_specs = [pl.BlockSpec((tm, d), row)] + [pl.BlockSpec((tm, wd), row) for wd in widths] + [
        pl.BlockSpec((None, 1, d), lambda i: (layer, 0, 0)),
        _resident((None, d, d_ff), lambda i: (layer, 0, 0)),
        _resident((None, d, d_ff), lambda i: (layer, 0, 1)),
        _resident((None, d_ff, d), lambda i: (layer, 0, 0)),
    ]
    args = [x, *parts, norm, w_in, w_in, w_out]
    if widths:
        in_specs.append(_resident((None, sum(widths), d), lambda i: (mix_index, 0, 0)))
        args.append(w_mix)
    return pl.pallas_call(
        functools.partial(_ffn_body, widths=widths),
        grid=(n // tm,),
        in_specs=in_specs,
        out_specs=pl.BlockSpec((tm, d), row),
        out_shape=jax.ShapeDtypeStruct((n, d), F32),
        compiler_params=_compiler_params(("parallel",), blocks, weights, temps),
        name="mix_ffn" if widths else "ffn",
    )(*args)


def _ab_in_body(x_ref, g_ref, w_ref, cw_ref, cb_ref, lg_ref, lb_ref, u_ref, q_ref, k_ref, v_ref,
                buf_ref, tail_ref, *, tm, c, rows):
    @pl.when(pl.program_id(1) == 0)
    def _():
        tail_ref[...] = jnp.zeros_like(tail_ref)

    h = _rms_to_bf16(x_ref[0], g_ref[...])
    glu = _dot(h, w_ref[:, 0:c]) * jax.nn.sigmoid(_dot(h, w_ref[:, c:2 * c]))
    q_ref[0] = (_dot(h, w_ref[:, 2 * c:3 * c]) * ATTN_SCALE).astype(BF16)
    k_ref[0] = _dot(h, w_ref[:, 3 * c:4 * c]).astype(BF16)
    v_ref[0] = _dot(h, w_ref[:, 4 * c:5 * c]).astype(BF16)

    buf_ref[0, 0:CONV_HALO, :] = tail_ref[...]
    buf_ref[0, CONV_HALO:, :] = glu
    tail_ref[...] = glu[tm - CONV_HALO:, :]
    moved = tm + CONV_HALO - V7X_SUBLANES
    for p in range(1, V7X_SUBLANES):
        buf_ref[p, 0:moved, :] = buf_ref[0, p:p + moved, :]
    first = CONV_HALO - (CONV_WIDTH - 1)
    for r in range(tm // rows):
        acc = jnp.broadcast_to(cb_ref[...], (rows, c))
        for tap in range(CONV_WIDTH):
            start = r * rows + first + tap
            base = start // V7X_SUBLANES * V7X_SUBLANES
            acc = acc + cw_ref[tap:tap + 1, :] * buf_ref[start - base, base:base + rows, :]
        mu = jnp.mean(acc, axis=-1, keepdims=True)
        xc = acc - mu
        var = jnp.mean(xc * xc, axis=-1, keepdims=True)
        y = xc * lax.rsqrt(var + EPS) * lg_ref[...] + lb_ref[...]
        u_ref[0, r * rows:(r + 1) * rows, :] = (y * jax.nn.sigmoid(y)).astype(BF16)


def _ab_in(x, norm, w, conv_w, conv_b, ln_g, ln_b, layer, e, *, tm, rows):
    bsz, s, d = x.shape
    width = w.shape[2]
    c = width // 5
    tok = lambda b, si: (b, si, 0)
    sel = lambda b, si: (e, 0, 0)
    vec = pl.BlockSpec((None, 1, c), sel)
    blocks = (_nbytes((tm, d), F32) + _nbytes((1, d), F32) + 4 * _nbytes((tm, c), BF16)
              + _nbytes((CONV_WIDTH + 1, c), F32) + 3 * _nbytes((V7X_SUBLANES, c), F32))
    buf_shape = (V7X_SUBLANES, tm + CONV_HALO, c)
    scratch = _nbytes(buf_shape, F32) + _nbytes((CONV_HALO, c), F32) + _nbytes((d, width), BF16)
    temps = 4 * _nbytes((tm, c), F32) + _nbytes((tm, d), BF16)
    return pl.pallas_call(
        functools.partial(_ab_in_body, tm=tm, c=c, rows=rows),
        grid=(bsz, s // tm),
        in_specs=[
            pl.BlockSpec((1, tm, d), tok),
            pl.BlockSpec((None, 1, d), lambda b, si: (layer, 0, 0)),
            _resident((None, d, width), sel),
            pl.BlockSpec((None, CONV_WIDTH, c), sel),
            vec, vec, vec,
        ],
        out_specs=[pl.BlockSpec((1, tm, c), tok)] * 4,
        out_shape=[jax.ShapeDtypeStruct((bsz, s, c), BF16)] * 4,
        scratch_shapes=[pltpu.VMEM(buf_shape, F32), pltpu.VMEM((CONV_HALO, c), F32)],
        compiler_params=_compiler_params(("parallel", "arbitrary"), blocks, scratch, temps),
        name="ab_in",
    )(x, norm, w, conv_w, conv_b, ln_g, ln_b)


SB_DEAD_MASS = 110.0
SB_MASKED_SCORE = -1e30


def _sb_body(q_ref, k_ref, v_ref, tri_ref, o_ref, acc_ref, run_ref, *, seq, chunk):
    tri = tri_ref[...]
    sub = tri.shape[0]

    def tile(r0, rows, s0, keys, diagonal, first):
        q = q_ref[0, r0:r0 + rows, :]
        zero = jnp.zeros_like(q)
        low = lax.broadcasted_iota(jnp.int32, q.shape, 1) < HEAD_DIM
        kblk = k_ref[0, s0:s0 + keys, :]
        vblk = v_ref[0, s0:s0 + keys, :]
        if diagonal:
            row = lax.broadcasted_iota(jnp.int32, (rows, keys), 0) + r0
            col = lax.broadcasted_iota(jnp.int32, (rows, keys), 1) + s0
            strict = col < row
        for hh in range(HEADS_PER_GROUP):
            q_head = jnp.where(low, q, zero) if hh == 0 else jnp.where(low, zero, q)
            z = lax.dot_general(q_head, kblk, _NT, preferred_element_type=F32)
            if diagonal:
                z = jnp.where(strict, z, SB_MASKED_SCORE)
            sp = _softplus(z)
            run = jnp.zeros((rows, V7X_LANES), F32) if first else run_ref[hh, r0:r0 + rows, :]
            parts = [None] * (keys // V7X_LANES)
            for c in reversed(range(keys // sub)):
                sp_c = sp[:, c * sub:(c + 1) * sub]
                after = _dot(sp_c.astype(BF16), tri)
                for g in range(sub // V7X_LANES):
                    lanes = slice(c * sub + g * V7X_LANES, c * sub + (g + 1) * V7X_LANES)
                    w = jnp.exp(z[:, lanes] - sp[:, lanes] - after[:, g * V7X_LANES:(g + 1) * V7X_LANES] - run)
                    parts[lanes.start // V7X_LANES] = w.astype(BF16)
                run = run + jnp.sum(sp_c, axis=1, keepdims=True)
            out = _dot(jnp.concatenate(parts, axis=1), vblk)
            if first:
                acc_ref[hh, r0:r0 + rows, :] = out
            else:
                acc_ref[hh, r0:r0 + rows, :] += out
            run_ref[hh, r0:r0 + rows, :] = run

    n_chunks = seq // chunk
    tile(0, chunk, 0, chunk, True, True)
    for c in range(1, n_chunks):
        tile(c * chunk, chunk, (c - 1) * chunk, 2 * chunk, True, True)

    if n_chunks > 2:
        @pl.when(jnp.min(run_ref[:, 2 * chunk:, :]) < SB_DEAD_MASS)
        def _():
            for kb in reversed(range(n_chunks - 2)):
                r0 = (kb + 2) * chunk
                tile(r0, seq - r0, kb * chunk, chunk, False, False)

    o_ref[0] = jnp.where(lax.broadcasted_iota(jnp.int32, (seq, V7X_LANES), 1) < HEAD_DIM,
                         acc_ref[0], acc_ref[1]).astype(BF16)


def _strict_lower_ones(n):
    j = lax.broadcasted_iota(jnp.int32, (n, n), 0)
    s = lax.broadcasted_iota(jnp.int32, (n, n), 1)
    return (j > s).astype(BF16)


def _sb_attention(q, k, v, *, chunk):
    bsz, s, hd = q.shape
    groups = hd // V7X_LANES
    assert s % chunk == 0
    spec = pl.BlockSpec((1, s, V7X_LANES), lambda b, p: (b, 0, p))
    blocks = 4 * _nbytes((s, V7X_LANES), BF16) + _nbytes((chunk, chunk), BF16)
    state = pltpu.VMEM((HEADS_PER_GROUP, s, V7X_LANES), F32)
    return pl.pallas_call(
        functools.partial(_sb_body, seq=s, chunk=chunk),
        grid=(bsz, groups),
        in_specs=[spec, spec, spec, pl.BlockSpec((chunk, chunk), lambda b, p: (0, 0))],
        out_specs=spec,
        out_shape=jax.ShapeDtypeStruct((bsz, s, hd), BF16),
        scratch_shapes=[state, state],
        compiler_params=_compiler_params(("parallel", "parallel"), blocks,
                                         2 * HEADS_PER_GROUP * _nbytes((s, V7X_LANES), F32),
                                         12 * _nbytes((s, chunk), F32)),
        name="sb_attention",
    )(q, k, v, _strict_lower_ones(chunk))


def _fox_in_body(x_ref, g_ref, w_ref, wft_ref, fb_ref, qg_ref, kg_ref, bd_ref, tri_ref,
                 q_ref, k_ref, v_ref, cum_ref, carry_ref, *, tm, hd):
    s = pl.program_id(1)
    h = _rms_to_bf16(x_ref[0], g_ref[...])
    bd = bd_ref[...]
    chunk = bd.shape[0]
    inv_dim = 1.0 / HEAD_DIM
    for base, gain_ref, out_ref, scale in ((0, qg_ref, q_ref, ATTN_SCALE), (hd, kg_ref, k_ref, 1.0)):
        proj = _dot(h, w_ref[:, base:base + hd])
        for c in range(hd // chunk):
            lanes = slice(c * chunk, (c + 1) * chunk)
            xg = proj[:, lanes]
            ss = _dot((xg * xg).astype(BF16), bd)
            y = (xg * lax.rsqrt(ss * inv_dim + EPS)) * gain_ref[...]
            out_ref[0, :, lanes] = (y * scale).astype(BF16)
    v_ref[0] = _dot(h, w_ref[:, 2 * hd:3 * hd]).astype(BF16)

    xf = lax.dot_general(wft_ref[...], h, _NT, preferred_element_type=F32) + fb_ref[...]
    logf = jnp.minimum(xf, 0.0) - jnp.log(1.0 + jnp.exp(-jnp.abs(xf)))
    p0 = logf.astype(BF16)
    r0 = logf - p0.astype(F32)
    p1 = r0.astype(BF16)
    p2 = (r0 - p1.astype(F32)).astype(BF16)
    tri = tri_ref[...]
    local = _dot(p0, tri) + _dot(p1, tri) + _dot(p2, tri)

    @pl.when(s == 0)
    def _():
        carry_ref[...] = jnp.zeros_like(carry_ref)

    cum = local + carry_ref[...]
    cum_ref[0] = cum
    carry_ref[...] = cum[:, tm - 1:tm]


NORM_CHUNK = V7X_MXU_DIM


def _head_block_ones():
    a = lax.broadcasted_iota(jnp.int32, (NORM_CHUNK, NORM_CHUNK), 0) // HEAD_DIM
    b = lax.broadcasted_iota(jnp.int32, (NORM_CHUNK, NORM_CHUNK), 1) // HEAD_DIM
    return (a == b).astype(BF16)


def _inclusive_upper_ones(n):
    j = lax.broadcasted_iota(jnp.int32, (n, n), 0)
    s = lax.broadcasted_iota(jnp.int32, (n, n), 1)
    return (j <= s).astype(BF16)


def _fox_in(x, norm, w, wft, fb, qg, kg, layer, o, *, tm):
    bsz, s, d = x.shape
    hd = w.shape[2] // 3
    heads = wft.shape[1]
    tok = lambda b, si: (b, si, 0)
    sel = lambda b, si: (o, 0, 0)
    blocks = (_nbytes((tm, d), F32) + _nbytes((d, 3 * hd), BF16) + _nbytes((heads, d), BF16)
              + 3 * _nbytes((tm, hd), BF16) + _nbytes((heads, tm), F32) + _nbytes((tm, tm), BF16)
              + _nbytes((NORM_CHUNK, NORM_CHUNK), BF16) + 4 * _nbytes((V7X_SUBLANES, d), F32))
    temps = _nbytes((tm, d), BF16) + 2 * _nbytes((tm, hd), F32) + 6 * _nbytes((tm, NORM_CHUNK), F32)
    return pl.pallas_call(
        functools.partial(_fox_in_body, tm=tm, hd=hd),
        grid=(bsz, s // tm),
        in_specs=[
            pl.BlockSpec((1, tm, d), tok),
            pl.BlockSpec((None, 1, d), lambda b, si: (layer, 0, 0)),
            pl.BlockSpec((None, d, 3 * hd), sel),
            pl.BlockSpec((None, heads, d), sel),
            pl.BlockSpec((None, heads, 1), sel),
            pl.BlockSpec((None, 1, NORM_CHUNK), sel),
            pl.BlockSpec((None, 1, NORM_CHUNK), sel),
            pl.BlockSpec((NORM_CHUNK, NORM_CHUNK), lambda b, si: (0, 0)),
            pl.BlockSpec((tm, tm), lambda b, si: (0, 0)),
        ],
        out_specs=[pl.BlockSpec((1, tm, hd), tok)] * 3 + [pl.BlockSpec((1, heads, tm), lambda b, si: (b, 0, si))],
        out_shape=[jax.ShapeDtypeStruct((bsz, s, hd), BF16)] * 3 + [jax.ShapeDtypeStruct((bsz, heads, s), F32)],
        scratch_shapes=[pltpu.VMEM((heads, 1), F32)],
        compiler_params=_compiler_params(("parallel", "arbitrary"), blocks,
                                         _nbytes((heads, V7X_LANES), F32), temps),
        name="fox_in",
    )(x, norm, w, wft, fb, qg, kg, _head_block_ones(), _inclusive_upper_ones(tm))


def _fox_body(q_ref, k_ref, v_ref, cum_ref, o_ref, vx_ref, acc_ref, m_ref, *, seq, tk):
    p = pl.program_id(1)
    v = v_ref[0]
    low = lax.broadcasted_iota(jnp.int32, v.shape, 1) < HEAD_DIM
    ones = jnp.ones_like(v)
    vx_ref[0] = jnp.where(low, v, ones)
    vx_ref[1] = jnp.where(low, ones, v)
    acc_ref[...] = jnp.zeros_like(acc_ref)
    m_ref[...] = jnp.full_like(m_ref, -jnp.inf)

    def tile(r0, rows, s0, keys, diagonal):
        q = q_ref[0, r0:r0 + rows, :]
        zero = jnp.zeros_like(q)
        low_q = lax.broadcasted_iota(jnp.int32, q.shape, 1) < HEAD_DIM
        kblk = k_ref[0, s0:s0 + keys, :]
        if diagonal:
            row = lax.broadcasted_iota(jnp.int32, (rows, keys), 0) + r0
            col = lax.broadcasted_iota(jnp.int32, (rows, keys), 1) + s0
            causal = col <= row
        for hh in range(HEADS_PER_GROUP):
            q_head = jnp.where(low_q, q, zero) if hh == 0 else jnp.where(low_q, zero, q)
            decay = cum_ref[0, pl.ds(p * HEADS_PER_GROUP + hh, 1), s0:s0 + keys]
            z = lax.dot_general(q_head, kblk, _NT, preferred_element_type=F32) - decay
            if diagonal:
                z = jnp.where(causal, z, -jnp.inf)
            m_old = m_ref[hh, r0:r0 + rows, :]
            m_new = jnp.maximum(m_old, jnp.max(z, axis=1, keepdims=True))
            alpha = jnp.exp(m_old - m_new)
            pr = jnp.concatenate(
                [jnp.exp(z[:, g * V7X_LANES:(g + 1) * V7X_LANES] - m_new) for g in range(keys // V7X_LANES)], axis=1)
            acc_ref[hh, r0:r0 + rows, :] = (alpha * acc_ref[hh, r0:r0 + rows, :]
                                            + _dot(pr.astype(BF16), vx_ref[hh, s0:s0 + keys, :]))
            m_ref[hh, r0:r0 + rows, :] = m_new

    for base in reversed(range(0, seq, tk)):
        tile(base, seq - base, base, tk, True)

    outs = [acc_ref[hh] / pltpu.roll(acc_ref[hh], HEAD_DIM, axis=1) for hh in range(HEADS_PER_GROUP)]
    o_ref[0] = jnp.where(low, outs[0], outs[1]).astype(BF16)


def _fox_attention(q, k, v, cum, *, tk):
    bsz, s, hd = q.shape
    heads = cum.shape[1]
    groups = hd // V7X_LANES
    assert s % tk == 0
    spec = pl.BlockSpec((1, s, V7X_LANES), lambda b, p: (b, 0, p))
    blocks = 4 * _nbytes((s, V7X_LANES), BF16) + _nbytes((heads, s), F32)
    state = pltpu.VMEM((HEADS_PER_GROUP, s, V7X_LANES), F32)
    return pl.pallas_call(
        functools.partial(_fox_body, seq=s, tk=tk),
        grid=(bsz, groups),
        in_specs=[spec, spec, spec, pl.BlockSpec((1, heads, s), lambda b, p: (b, 0, 0))],
        out_specs=spec,
        out_shape=jax.ShapeDtypeStruct((bsz, s, hd), BF16),
        scratch_shapes=[pltpu.VMEM((HEADS_PER_GROUP, s, V7X_LANES), BF16), state, state],
        compiler_params=_compiler_params(("parallel", "parallel"), blocks,
                                         HEADS_PER_GROUP * (_nbytes((s, V7X_LANES), BF16)
                                                            + 2 * _nbytes((s, V7X_LANES), F32)),
                                         12 * _nbytes((s, tk), F32)),
        name="fox_attention",
    )(q, k, v, cum)


FFN_ROWS = 512
CONV_ACC_ROWS = 64
FOX_KEYS = 512
SB_CHUNK = V7X_MXU_DIM


def kernel(x, ffn1_norm, ffn1_w_in, ffn1_w_out, mix_norm, ffn2_norm, ffn2_w_in, ffn2_w_out, ab_w_in, conv_w, conv_b,
           conv_ln_g, conv_ln_b, ab_w_out, fox_w_in, fox_f_bias, fox_q_norm, fox_k_norm, fox_w_out):
    bsz, s, d = x.shape
    depth = ffn1_norm.shape[0]
    n = bsz * s
    row3 = lambda a: a.reshape(a.shape[0], 1, a.shape[1])

    ffn1_norm, ffn2_norm, mix_norm = row3(ffn1_norm), row3(ffn2_norm), row3(mix_norm)
    conv_b, conv_ln_g, conv_ln_b = row3(conv_b), row3(conv_ln_g), row3(conv_ln_b)
    ffn1_w_in, ffn1_w_out = ffn1_w_in.astype(BF16), ffn1_w_out.astype(BF16)
    ffn2_w_in, ffn2_w_out = ffn2_w_in.astype(BF16), ffn2_w_out.astype(BF16)
    ab_w_in, ab_w_out = ab_w_in.astype(BF16), ab_w_out.astype(BF16)
    fox_w_out = fox_w_out.astype(BF16)
    hd = fox_w_out.shape[1]
    fox_w_qkv = fox_w_in[:, :, :3 * hd].astype(BF16)
    fox_w_ft = jnp.swapaxes(fox_w_in[:, :, 3 * hd:], 1, 2).astype(BF16)
    fox_fb = fox_f_bias[:, :, None]
    fox_qg = row3(jnp.tile(fox_q_norm, (1, NORM_CHUNK // HEAD_DIM)))
    fox_kg = row3(jnp.tile(fox_k_norm, (1, NORM_CHUNK // HEAD_DIM)))

    xf = x.reshape(n, d)
    for layer in range(depth):
        xf = _ffn(xf, ffn1_norm, ffn1_w_in, ffn1_w_out, layer, tm=FFN_ROWS)
        if layer % 2 == 0:
            e = layer // 2
            u, q, k, v = _ab_in(xf.reshape(bsz, s, d), mix_norm, ab_w_in, conv_w, conv_b, conv_ln_g, conv_ln_b,
                                layer, e, tm=FFN_ROWS, rows=CONV_ACC_ROWS)
            att = _sb_attention(q, k, v, chunk=SB_CHUNK)
            c = u.shape[2]
            parts, w_mix, mix_index = [u.reshape(n, c), att.reshape(n, c)], ab_w_out, e
        else:
            o = layer // 2
            q, k, v, cum = _fox_in(xf.reshape(bsz, s, d), mix_norm, fox_w_qkv, fox_w_ft, fox_fb, fox_qg, fox_kg,
                                   layer, o, tm=FFN_ROWS)
            att = _fox_attention(q, k, v, cum, tk=FOX_KEYS)
            parts, w_mix, mix_index = [att.reshape(n, hd)], fox_w_out, o
        xf = _ffn(xf, ffn2_norm, ffn2_w_in, ffn2_w_out, layer, tm=FFN_ROWS,
                  parts=parts, w_mix=w_mix, mix_index=mix_index)
    return xf.reshape(bsz, s, d)
```

```python
import functools
import math

import jax
import jax.numpy as jnp
from jax import lax
from jax.experimental import pallas as pl
from jax.experimental.pallas import tpu as pltpu

F32 = jnp.float32
BF16 = jnp.bfloat16

EPS = 1e-6
HEAD_DIM = 64
CONV_WIDTH = 31
ATTN_SCALE = HEAD_DIM ** -0.5
LOG2_E = math.log2(math.e)

V7X_LANES = 128
V7X_SUBLANES = 8
V7X_MXU_DIM = 256
V7X_VMEM_BYTES = 64 * 1024 * 1024

HEADS_PER_GROUP = V7X_LANES // HEAD_DIM
CONV_HALO = 32

_NT = (((1,), (1,)), ((), ()))


def _nbytes(shape, dtype):
    n = 1
    for d in shape:
        n *= d
    return n * jnp.dtype(dtype).itemsize


def _compiler_params(semantics, block_bytes, scratch_bytes, temp_bytes):
    need = 2 * block_bytes + scratch_bytes + temp_bytes
    mib = 1024 * 1024
    limit = min(-(-need // mib) * mib, V7X_VMEM_BYTES - 4 * mib)
    return pltpu.CompilerParams(dimension_semantics=semantics, vmem_limit_bytes=limit)


def _dot(a, b):
    return jnp.dot(a, b, preferred_element_type=F32)


def _rms_to_bf16(x, g):
    ms = jnp.mean(x * x, axis=-1, keepdims=True)
    return ((x * lax.rsqrt(ms + EPS)) * g).astype(BF16)


def _softplus(z):
    return jnp.maximum(z, 0.0) + jnp.log(1.0 + jnp.exp2(jnp.abs(z) * -LOG2_E))


def _ffn_body(*refs, widths):
    x_ref = refs[0]
    part_refs = refs[1:1 + len(widths)]
    g_ref, wg_ref, wu_ref, wo_ref = refs[1 + len(widths):5 + len(widths)]
    o_ref = refs[-1]
    x = x_ref[...]
    if widths:
        wm_ref = refs[5 + len(widths)]
        off = 0
        for part_ref, width in zip(part_refs, widths):
            x = x + _dot(part_ref[...], wm_ref[off:off + width, :])
            off += width
    h = _rms_to_bf16(x, g_ref[...])
    gate = _dot(h, wg_ref[...])
    up = _dot(h, wu_ref[...])
    act = (gate * jax.nn.sigmoid(gate) * up).astype(BF16)
    o_ref[...] = x + 0.5 * _dot(act, wo_ref[...])


def _resident(block_shape, index_map):
    return pl.BlockSpec(block_shape, index_map, pipeline_mode=pl.Buffered(1))


def _ffn(x, norm, w_in, w_out, layer, *, tm, parts=(), w_mix=None, mix_index=0):
    n, d = x.shape
    d_ff = w_out.shape[1]
    widths = tuple(p.shape[1] for p in parts)
    row = lambda i: (i, 0)
    blocks = _nbytes((tm, d), F32) * 2 + _nbytes((1, d), F32) + sum(_nbytes((tm, wd), BF16) for wd in widths)
    weights = _nbytes((d, d_ff), BF16) * 3 + _nbytes((sum(widths), d), BF16)
    temps = 2 * _nbytes((tm, d_ff), F32) + _nbytes((tm, d_ff), BF16) + 2 * _nbytes((tm, d), F32)
    in_specs = [pl.BlockSpec((tm, d), row)] + [pl.BlockSpec((tm, wd), row) for wd in widths] + [
        pl.BlockSpec((None, 1, d), lambda i: (layer, 0, 0)),
        _resident((None, d, d_ff), lambda i: (layer, 0, 0)),
        _resident((None, d, d_ff), lambda i: (layer, 0, 1)),
        _resident((None, d_ff, d), lambda i: (layer, 0, 0)),
    ]
    args = [x, *parts, norm, w_in, w_in, w_out]
    if widths:
        in_specs.append(_resident((None, sum(widths), d), lambda i: (mix_index, 0, 0)))
        args.append(w_mix)
    return pl.pallas_call(
        functools.partial(_ffn_body, widths=widths),
        grid=(n // tm,),
        in_specs=in_specs,
        out_specs=pl.BlockSpec((tm, d), row),
        out_shape=jax.ShapeDtypeStruct((n, d), F32),
        compiler_params=_compiler_params(("parallel",), blocks, weights, temps),
        name="mix_ffn" if widths else "ffn",
    )(*args)


def _ab_in_body(x_ref, g_ref, w_ref, cw_ref, cb_ref, lg_ref, lb_ref, u_ref, q_ref, k_ref, v_ref,
                buf_ref, tail_ref, *, tm, c, rows):
    @pl.when(pl.program_id(1) == 0)
    def _():
        tail_ref[...] = jnp.zeros_like(tail_ref)

    h = _rms_to_bf16(x_ref[0], g_ref[...])
    glu = _dot(h, w_ref[:, 0:c]) * jax.nn.sigmoid(_dot(h, w_ref[:, c:2 * c]))
    q_ref[0] = (_dot(h, w_ref[:, 2 * c:3 * c]) * ATTN_SCALE).astype(BF16)
    k_ref[0] = _dot(h, w_ref[:, 3 * c:4 * c]).astype(BF16)
    v_ref[0] = _dot(h, w_ref[:, 4 * c:5 * c]).astype(BF16)

    buf_ref[0, 0:CONV_HALO, :] = tail_ref[...]
    buf_ref[0, CONV_HALO:, :] = glu
    tail_ref[...] = glu[tm - CONV_HALO:, :]
    moved = tm + CONV_HALO - V7X_SUBLANES
    for p in range(1, V7X_SUBLANES):
        buf_ref[p, 0:moved, :] = buf_ref[0, p:p + moved, :]
    first = CONV_HALO - (CONV_WIDTH - 1)
    for r in range(tm // rows):
        acc = jnp.broadcast_to(cb_ref[...], (rows, c))
        for tap in range(CONV_WIDTH):
            start = r * rows + first + tap
            base = start // V7X_SUBLANES * V7X_SUBLANES
            acc = acc + cw_ref[tap:tap + 1, :] * buf_ref[start - base, base:base + rows, :]
        mu = jnp.mean(acc, axis=-1, keepdims=True)
        xc = acc - mu
        var = jnp.mean(xc * xc, axis=-1, keepdims=True)
        y = xc * lax.rsqrt(var + EPS) * lg_ref[...] + lb_ref[...]
        u_ref[0, r * rows:(r + 1) * rows, :] = (y * jax.nn.sigmoid(y)).astype(BF16)


def _ab_in(x, norm, w, conv_w, conv_b, ln_g, ln_b, layer, e, *, tm, rows):
    bsz, s, d = x.shape
    width = w.shape[2]
    c = width // 5
    tok = lambda b, si: (b, si, 0)
    sel = lambda b, si: (e, 0, 0)
    vec = pl.BlockSpec((None, 1, c), sel)
    blocks = (_nbytes((tm, d), F32) + _nbytes((1, d), F32) + 4 * _nbytes((tm, c), BF16)
              + _nbytes((CONV_WIDTH + 1, c), F32) + 3 * _nbytes((V7X_SUBLANES, c), F32))
    buf_shape = (V7X_SUBLANES, tm + CONV_HALO, c)
    scratch = _nbytes(buf_shape, F32) + _nbytes((CONV_HALO, c), F32) + _nbytes((d, width), BF16)
    temps = 4 * _nbytes((tm, c), F32) + _nbytes((tm, d), BF16)
    return pl.pallas_call(
        functools.partial(_ab_in_body, tm=tm, c=c, rows=rows),
        grid=(bsz, s // tm),
        in_specs=[
            pl.BlockSpec((1, tm, d), tok),
            pl.BlockSpec((None, 1, d), lambda b, si: (layer, 0, 0)),
            _resident((None, d, width), sel),
            pl.BlockSpec((None, CONV_WIDTH, c), sel),
            vec, vec, vec,
        ],
        out_specs=[pl.BlockSpec((1, tm, c), tok)] * 4,
        out_shape=[jax.ShapeDtypeStruct((bsz, s, c), BF16)] * 4,
        scratch_shapes=[pltpu.VMEM(buf_shape, F32), pltpu.VMEM((CONV_HALO, c), F32)],
        compiler_params=_compiler_params(("parallel", "arbitrary"), blocks, scratch, temps),
        name="ab_in",
    )(x, norm, w, conv_w, conv_b, ln_g, ln_b)


SB_DEAD_MASS = 110.0
SB_MASKED_SCORE = -1e30


def _sb_body(q_ref, k_ref, v_ref, tri_ref, o_ref, acc_ref, run_ref, *, seq, chunk):
    tri = tri_ref[...]
    sub = tri.shape[0]

    def tile(r0, rows, s0, keys, diagonal, first):
        q = q_ref[0, r0:r0 + rows, :]
        zero = jnp.zeros_like(q)
        low = lax.broadcasted_iota(jnp.int32, q.shape, 1) < HEAD_DIM
        kblk = k_ref[0, s0:s0 + keys, :]
        vblk = v_ref[0, s0:s0 + keys, :]
        if diagonal:
            row = lax.broadcasted_iota(jnp.int32, (rows, keys), 0) + r0
            col = lax.broadcasted_iota(jnp.int32, (rows, keys), 1) + s0
            strict = col < row
        for hh in range(HEADS_PER_GROUP):
            q_head = jnp.where(low, q, zero) if hh == 0 else jnp.where(low, zero, q)
            z = lax.dot_general(q_head, kblk, _NT, preferred_element_type=F32)
            if diagonal:
                z = jnp.where(strict, z, SB_MASKED_SCORE)
            sp = _softplus(z)
            run = jnp.zeros((rows, V7X_LANES), F32) if first else run_ref[hh, r0:r0 + rows, :]
            parts = [None] * (keys // V7X_LANES)
            for c in reversed(range(keys // sub)):
                sp_c = sp[:, c * sub:(c + 1) * sub]
                after = _dot(sp_c.astype(BF16), tri)
                for g in range(sub // V7X_LANES):
                    lanes = slice(c * sub + g * V7X_LANES, c * sub + (g + 1) * V7X_LANES)
                    w = jnp.exp(z[:, lanes] - sp[:, lanes] - after[:, g * V7X_LANES:(g + 1) * V7X_LANES] - run)
                    parts[lanes.start // V7X_LANES] = w.astype(BF16)
                run = run + jnp.sum(sp_c, axis=1, keepdims=True)
            out = _dot(jnp.concatenate(parts, axis=1), vblk)
            if first:
                acc_ref[hh, r0:r0 + rows, :] = out
            else:
                acc_ref[hh, r0:r0 + rows, :] += out
            run_ref[hh, r0:r0 + rows, :] = run

    n_chunks = seq // chunk
    tile(0, chunk, 0, chunk, True, True)
    for c in range(1, n_chunks):
        tile(c * chunk, chunk, (c - 1) * chunk, 2 * chunk, True, True)

    if n_chunks > 2:
        @pl.when(jnp.min(run_ref[:, 2 * chunk:, :]) < SB_DEAD_MASS)
        def _():
            for kb in reversed(range(n_chunks - 2)):
                r0 = (kb + 2) * chunk
                tile(r0, seq - r0, kb * chunk, chunk, False, False)

    o_ref[0] = jnp.where(lax.broadcasted_iota(jnp.int32, (seq, V7X_LANES), 1) < HEAD_DIM,
                         acc_ref[0], acc_ref[1]).astype(BF16)


def _strict_lower_ones(n):
    j = lax.broadcasted_iota(jnp.int32, (n, n), 0)
    s = lax.broadcasted_iota(jnp.int32, (n, n), 1)
    return (j > s).astype(BF16)


def _sb_attention(q, k, v, *, chunk):
    bsz, s, hd = q.shape
    groups = hd // V7X_LANES
    assert s % chunk == 0
    spec = pl.BlockSpec((1, s, V7X_LANES), lambda b, p: (b, 0, p))
    blocks = 4 * _nbytes((s, V7X_LANES), BF16) + _nbytes((chunk, chunk), BF16)
    state = pltpu.VMEM((HEADS_PER_GROUP, s, V7X_LANES), F32)
    return pl.pallas_call(
        functools.partial(_sb_body, seq=s, chunk=chunk),
        grid=(bsz, groups),
        in_specs=[spec, spec, spec, pl.BlockSpec((chunk, chunk), lambda b, p: (0, 0))],
        out_specs=spec,
        out_shape=jax.ShapeDtypeStruct((bsz, s, hd), BF16),
        scratch_shapes=[state, state],
        compiler_params=_compiler_params(("parallel", "parallel"), blocks,
                                         2 * HEADS_PER_GROUP * _nbytes((s, V7X_LANES), F32),
                                         12 * _nbytes((s, chunk), F32)),
        name="sb_attention",
    )(q, k, v, _strict_lower_ones(chunk))


def _fox_in_body(x_ref, g_ref, w_ref, wft_ref, fb_ref, qg_ref, kg_ref, bd_ref, tri_ref,
                 q_ref, k_ref, v_ref, cum_ref, carry_ref, *, tm, hd):
    s = pl.program_id(1)
    h = _rms_to_bf16(x_ref[0], g_ref[...])
    bd = bd_ref[...]
    chunk = bd.shape[0]
    inv_dim = 1.0 / HEAD_DIM
    for base, gain_ref, out_ref, scale in ((0, qg_ref, q_ref, ATTN_SCALE), (hd, kg_ref, k_ref, 1.0)):
        proj = _dot(h, w_ref[:, base:base + hd])
        for c in range(hd // chunk):
            lanes = slice(c * chunk, (c + 1) * chunk)
            xg = proj[:, lanes]
            ss = _dot((xg * xg).astype(BF16), bd)
            y = (xg * lax.rsqrt(ss * inv_dim + EPS)) * gain_ref[...]
            out_ref[0, :, lanes] = (y * scale).astype(BF16)
    v_ref[0] = _dot(h, w_ref[:, 2 * hd:3 * hd]).astype(BF16)

    xf = lax.dot_general(wft_ref[...], h, _NT, preferred_element_type=F32) + fb_ref[...]
    logf = jnp.minimum(xf, 0.0) - jnp.log(1.0 + jnp.exp(-jnp.abs(xf)))
    p0 = logf.astype(BF16)
    r0 = logf - p0.astype(F32)
    p1 = r0.astype(BF16)
    p2 = (r0 - p1.astype(F32)).astype(BF16)
    tri = tri_ref[...]
    local = _dot(p0, tri) + _dot(p1, tri) + _dot(p2, tri)

    @pl.when(s == 0)
    def _():
        carry_ref[...] = jnp.zeros_like(carry_ref)

    cum = local + carry_ref[...]
    cum_ref[0] = cum
    carry_ref[...] = cum[:, tm - 1:tm]


NORM_CHUNK = V7X_MXU_DIM


def _head_block_ones():
    a = lax.broadcasted_iota(jnp.int32, (NORM_CHUNK, NORM_CHUNK), 0) // HEAD_DIM
    b = lax.broadcasted_iota(jnp.int32, (NORM_CHUNK, NORM_CHUNK), 1) // HEAD_DIM
    return (a == b).astype(BF16)


def _inclusive_upper_ones(n):
    j = lax.broadcasted_iota(jnp.int32, (n, n), 0)
    s = lax.broadcasted_iota(jnp.int32, (n, n), 1)
    return (j <= s).astype(BF16)


def _fox_in(x, norm, w, wft, fb, qg, kg, layer, o, *, tm):
    bsz, s, d = x.shape
    hd = w.shape[2] // 3
    heads = wft.shape[1]
    tok = lambda b, si: (b, si, 0)
    sel = lambda b, si: (o, 0, 0)
    blocks = (_nbytes((tm, d), F32) + _nbytes((d, 3 * hd), BF16) + _nbytes((heads, d), BF16)
              + 3 * _nbytes((tm, hd), BF16) + _nbytes((heads, tm), F32) + _nbytes((tm, tm), BF16)
              + _nbytes((NORM_CHUNK, NORM_CHUNK), BF16) + 4 * _nbytes((V7X_SUBLANES, d), F32))
    temps = _nbytes((tm, d), BF16) + 2 * _nbytes((tm, hd), F32) + 6 * _nbytes((tm, NORM_CHUNK), F32)
    return pl.pallas_call(
        functools.partial(_fox_in_body, tm=tm, hd=hd),
        grid=(bsz, s // tm),
        in_specs=[
            pl.BlockSpec((1, tm, d), tok),
            pl.BlockSpec((None, 1, d), lambda b, si: (layer, 0, 0)),
            pl.BlockSpec((None, d, 3 * hd), sel),
            pl.BlockSpec((None, heads, d), sel),
            pl.BlockSpec((None, heads, 1), sel),
            pl.BlockSpec((None, 1, NORM_CHUNK), sel),
            pl.BlockSpec((None, 1, NORM_CHUNK), sel),
            pl.BlockSpec((NORM_CHUNK, NORM_CHUNK), lambda b, si: (0, 0)),
            pl.BlockSpec((tm, tm), lambda b, si: (0, 0)),
        ],
        out_specs=[pl.BlockSpec((1, tm, hd), tok)] * 3 + [pl.BlockSpec((1, heads, tm), lambda b, si: (b, 0, si))],
        out_shape=[jax.ShapeDtypeStruct((bsz, s, hd), BF16)] * 3 + [jax.ShapeDtypeStruct((bsz, heads, s), F32)],
        scratch_shapes=[pltpu.VMEM((heads, 1), F32)],
        compiler_params=_compiler_params(("parallel", "arbitrary"), blocks,
                                         _nbytes((heads, V7X_LANES), F32), temps),
        name="fox_in",
    )(x, norm, w, wft, fb, qg, kg, _head_block_ones(), _inclusive_upper_ones(tm))


def _fox_body(q_ref, k_ref, v_ref, cum_ref, o_ref, vx_ref, acc_ref, m_ref, *, seq, tk):
    p = pl.program_id(1)
    v = v_ref[0]
    low = lax.broadcasted_iota(jnp.int32, v.shape, 1) < HEAD_DIM
    ones = jnp.ones_like(v)
    vx_ref[0] = jnp.where(low, v, ones)
    vx_ref[1] = jnp.where(low, ones, v)
    acc_ref[...] = jnp.zeros_like(acc_ref)
    m_ref[...] = jnp.full_like(m_ref, -jnp.inf)

    def tile(r0, rows, s0, keys, diagonal):
        q = q_ref[0, r0:r0 + rows, :]
        zero = jnp.zeros_like(q)
        low_q = lax.broadcasted_iota(jnp.int32, q.shape, 1) < HEAD_DIM
        kblk = k_ref[0, s0:s0 + keys, :]
        if diagonal:
            row = lax.broadcasted_iota(jnp.int32, (rows, keys), 0) + r0
            col = lax.broadcasted_iota(jnp.int32, (rows, keys), 1) + s0
            causal = col <= row
        for hh in range(HEADS_PER_GROUP):
            q_head = jnp.where(low_q, q, zero) if hh == 0 else jnp.where(low_q, zero, q)
            decay = cum_ref[0, pl.ds(p * HEADS_PER_GROUP + hh, 1), s0:s0 + keys]
            z = lax.dot_general(q_head, kblk, _NT, preferred_element_type=F32) - decay
            if diagonal:
                z = jnp.where(causal, z, -jnp.inf)
            m_old = m_ref[hh, r0:r0 + rows, :]
            m_new = jnp.maximum(m_old, jnp.max(z, axis=1, keepdims=True))
            alpha = jnp.exp(m_old - m_new)
            pr = jnp.concatenate(
                [jnp.exp(z[:, g * V7X_LANES:(g + 1) * V7X_LANES] - m_new) for g in range(keys // V7X_LANES)], axis=1)
            acc_ref[hh, r0:r0 + rows, :] = (alpha * acc_ref[hh, r0:r0 + rows, :]
                                            + _dot(pr.astype(BF16), vx_ref[hh, s0:s0 + keys, :]))
            m_ref[hh, r0:r0 + rows, :] = m_new

    for base in range(0, seq, tk):
        tile(base, seq - base, base, tk, True)

    outs = [acc_ref[hh] / pltpu.roll(acc_ref[hh], HEAD_DIM, axis=1) for hh in range(HEADS_PER_GROUP)]
    o_ref[0] = jnp.where(low, outs[0], outs[1]).astype(BF16)


def _fox_attention(q, k, v, cum, *, tk):
    bsz, s, hd = q.shape
    heads = cum.shape[1]
    groups = hd // V7X_LANES
    assert s % tk == 0
    spec = pl.BlockSpec((1, s, V7X_LANES), lambda b, p: (b, 0, p))
    blocks = 4 * _nbytes((s, V7X_LANES), BF16) + _nbytes((heads, s), F32)
    state = pltpu.VMEM((HEADS_PER_GROUP, s, V7X_LANES), F32)
    return pl.pallas_call(
        functools.partial(_fox_body, seq=s, tk=tk),
        grid=(bsz, groups),
        in_specs=[spec, spec, spec, pl.BlockSpec((1, heads, s), lambda b, p: (b, 0, 0))],
        out_specs=spec,
        out_shape=jax.ShapeDtypeStruct((bsz, s, hd), BF16),
        scratch_shapes=[pltpu.VMEM((HEADS_PER_GROUP, s, V7X_LANES), BF16), state, state],
        compiler_params=_compiler_params(("parallel", "parallel"), blocks,
                                         HEADS_PER_GROUP * (_nbytes((s, V7X_LANES), BF16)
                                                            + 2 * _nbytes((s, V7X_LANES), F32)),
                                         12 * _nbytes((s, tk), F32)),
        name="fox_attention",
    )(q, k, v, cum)


FFN_ROWS = 512
CONV_ACC_ROWS = 512
FOX_KEYS = 512
SB_CHUNK = V7X_MXU_DIM


def kernel(x, ffn1_norm, ffn1_w_in, ffn1_w_out, mix_norm, ffn2_norm, ffn2_w_in, ffn2_w_out, ab_w_in, conv_w, conv_b,
           conv_ln_g, conv_ln_b, ab_w_out, fox_w_in, fox_f_bias, fox_q_norm, fox_k_norm, fox_w_out):
    bsz, s, d = x.shape
    depth = ffn1_norm.shape[0]
    n = bsz * s
    row3 = lambda a: a.reshape(a.shape[0], 1, a.shape[1])

    ffn1_norm, ffn2_norm, mix_norm = row3(ffn1_norm), row3(ffn2_norm), row3(mix_norm)
    conv_b, conv_ln_g, conv_ln_b = row3(conv_b), row3(conv_ln_g), row3(conv_ln_b)
    ffn1_w_in, ffn1_w_out = ffn1_w_in.astype(BF16), ffn1_w_out.astype(BF16)
    ffn2_w_in, ffn2_w_out = ffn2_w_in.astype(BF16), ffn2_w_out.astype(BF16)
    ab_w_in, ab_w_out = ab_w_in.astype(BF16), ab_w_out.astype(BF16)
    fox_w_out = fox_w_out.astype(BF16)
    hd = fox_w_out.shape[1]
    fox_w_qkv = fox_w_in[:, :, :3 * hd].astype(BF16)
    fox_w_ft = jnp.swapaxes(fox_w_in[:, :, 3 * hd:], 1, 2).astype(BF16)
    fox_fb = fox_f_bias[:, :, None]
    fox_qg = row3(jnp.tile(fox_q_norm, (1, NORM_CHUNK // HEAD_DIM)))
    fox_kg = row3(jnp.tile(fox_k_norm, (1, NORM_CHUNK // HEAD_DIM)))

    xf = x.reshape(n, d)
    for layer in range(depth):
        xf = _ffn(xf, ffn1_norm, ffn1_w_in, ffn1_w_out, layer, tm=FFN_ROWS)
        if layer % 2 == 0:
            e = layer // 2
            u, q, k, v = _ab_in(xf.reshape(bsz, s, d), mix_norm, ab_w_in, conv_w, conv_b, conv_ln_g, conv_ln_b,
                                layer, e, tm=FFN_ROWS, rows=CONV_ACC_ROWS)
            att = _sb_attention(q, k, v, chunk=SB_CHUNK)
            c = u.shape[2]
            parts, w_mix, mix_index = [u.reshape(n, c), att.reshape(n, c)], ab_w_out, e
        else:
            o = layer // 2
            q, k, v, cum = _fox_in(xf.reshape(bsz, s, d), mix_norm, fox_w_qkv, fox_w_ft, fox_fb, fox_qg, fox_kg,
                                   layer, o, tm=FFN_ROWS)
            att = _fox_attention(q, k, v, cum, tk=FOX_KEYS)
            parts, w_mix, mix_index = [att.reshape(n, hd)], fox_w_out, o
        xf = _ffn(xf, ffn2_norm, ffn2_w_in, ffn2_w_out, layer, tm=FFN_ROWS,
                  parts=parts, w_mix=w_mix, mix_index=mix_index)
    return xf.reshape(bsz, s, d)
```

```python
import functools
import math

import jax
import jax.numpy as jnp
from jax import lax
from jax.experimental import pallas as pl
from jax.experimental.pallas import tpu as pltpu

F32 = jnp.float32
BF16 = jnp.bfloat16

EPS = 1e-6
HEAD_DIM = 64
CONV_WIDTH = 31
ATTN_SCALE = HEAD_DIM ** -0.5
LOG2_E = math.log2(math.e)

V7X_LANES = 128
V7X_SUBLANES = 8
V7X_MXU_DIM = 256
V7X_VMEM_BYTES = 64 * 1024 * 1024

HEADS_PER_GROUP = V7X_LANES // HEAD_DIM
CONV_HALO = 32

_NT = (((1,), (1,)), ((), ()))


def _nbytes(shape, dtype):
    n = 1
    for d in shape:
        n *= d
    return n * jnp.dtype(dtype).itemsize


def _compiler_params(semantics, block_bytes, scratch_bytes, temp_bytes):
    need = 2 * block_bytes + scratch_bytes + temp_bytes
    mib = 1024 * 1024
    limit = min(-(-need // mib) * mib, V7X_VMEM_BYTES - 4 * mib)
    return pltpu.CompilerParams(dimension_semantics=semantics, vmem_limit_bytes=limit)


def _dot(a, b):
    return jnp.dot(a, b, preferred_element_type=F32)


def _rms_to_bf16(x, g):
    ms = jnp.mean(x * x, axis=-1, keepdims=True)
    return ((x * lax.rsqrt(ms + EPS)) * g).astype(BF16)


def _softplus(z):
    return jnp.maximum(z, 0.0) + jnp.log(1.0 + jnp.exp2(jnp.abs(z) * -LOG2_E))


def _ffn_body(*refs, widths):
    x_ref = refs[0]
    part_refs = refs[1:1 + len(widths)]
    g_ref, wg_ref, wu_ref, wo_ref = refs[1 + len(widths):5 + len(widths)]
    o_ref = refs[-1]
    x = x_ref[...]
    if widths:
        wm_ref = refs[5 + len(widths)]
        off = 0
        for part_ref, width in zip(part_refs, widths):
            x = x + _dot(part_ref[...], wm_ref[off:off + width, :])
            off += width
    h = _rms_to_bf16(x, g_ref[...])
    gate = _dot(h, wg_ref[...])
    up = _dot(h, wu_ref[...])
    act = (gate * jax.nn.sigmoid(gate) * up).astype(BF16)
    o_ref[...] = x + 0.5 * _dot(act, wo_ref[...])


def _resident(block_shape, index_map):
    return pl.BlockSpec(block_shape, index_map, pipeline_mode=pl.Buffered(1))


def _ffn(x, norm, w_in, w_out, layer, *, tm, parts=(), w_mix=None, mix_index=0):
    n, d = x.shape
    d_ff = w_out.shape[1]
    widths = tuple(p.shape[1] for p in parts)
    row = lambda i: (i, 0)
    blocks = _nbytes((tm, d), F32) * 2 + _nbytes((1, d), F32) + sum(_nbytes((tm, wd), BF16) for wd in widths)
    weights = _nbytes((d, d_ff), BF16) * 3 + _nbytes((sum(widths), d), BF16)
    temps = 2 * _nbytes((tm, d_ff), F32) + _nbytes((tm, d_ff), BF16) + 2 * _nbytes((tm, d), F32)
    in_specs = [pl.BlockSpec((tm, d), row)] + [pl.BlockSpec((tm, wd), row) for wd in widths] + [
        pl.BlockSpec((None, 1, d), lambda i: (layer, 0, 0)),
        _resident((None, d, d_ff), lambda i: (layer, 0, 0)),
        _resident((None, d, d_ff), lambda i: (layer, 0, 1)),
        _resident((None, d_ff, d), lambda i: (layer, 0, 0)),
    ]
    args = [x, *parts, norm, w_in, w_in, w_out]
    if widths:
        in_specs.append(_resident((None, sum(widths), d), lambda i: (mix_index, 0, 0)))
        args.append(w_mix)
    return pl.pallas_call(
        functools.partial(_ffn_body, widths=widths),
        grid=(n // tm,),
        in_specs=in_specs,
        out_specs=pl.BlockSpec((tm, d), row),
        out_shape=jax.ShapeDtypeStruct((n, d), F32),
        compiler_params=_compiler_params(("parallel",), blocks, weights, temps),
        name="mix_ffn" if widths else "ffn",
    )(*args)


def _ab_in_body(x_ref, g_ref, w_ref, cw_ref, cb_ref, lg_ref, lb_ref, u_ref, q_ref, k_ref, v_ref,
                buf_ref, tail_ref, *, tm, c, rows):
    @pl.when(pl.program_id(1) == 0)
    def _():
        tail_ref[...] = jnp.zeros_like(tail_ref)

    h = _rms_to_bf16(x_ref[0], g_ref[...])
    glu = _dot(h, w_ref[:, 0:c]) * jax.nn.sigmoid(_dot(h, w_ref[:, c:2 * c]))
    q_ref[0] = (_dot(h, w_ref[:, 2 * c:3 * c]) * ATTN_SCALE).astype(BF16)
    k_ref[0] = _dot(h, w_ref[:, 3 * c:4 * c]).astype(BF16)
    v_ref[0] = _dot(h, w_ref[:, 4 * c:5 * c]).astype(BF16)

    buf_ref[0, 0:CONV_HALO, :] = tail_ref[...]
    buf_ref[0, CONV_HALO:, :] = glu
    tail_ref[...] = glu[tm - CONV_HALO:, :]
    moved = tm + CONV_HALO - V7X_SUBLANES
    for p in range(1, V7X_SUBLANES):
        buf_ref[p, 0:moved, :] = buf_ref[0, p:p + moved, :]
    first = CONV_HALO - (CONV_WIDTH - 1)
    for r in range(tm // rows):
        acc = jnp.broadcast_to(cb_ref[...], (rows, c))
        for tap in range(CONV_WIDTH):
            start = r * rows + first + tap
            base = start // V7X_SUBLANES * V7X_SUBLANES
            acc = acc + cw_ref[tap:tap + 1, :] * buf_ref[start - base, base:base + rows, :]
        mu = jnp.mean(acc, axis=-1, keepdims=True)
        xc = acc - mu
        var = jnp.mean(xc * xc, axis=-1, keepdims=True)
        y = xc * lax.rsqrt(var + EPS) * lg_ref[...] + lb_ref[...]
        u_ref[0, r * rows:(r + 1) * rows, :] = (y * jax.nn.sigmoid(y)).astype(BF16)


def _ab_in(x, norm, w, conv_w, conv_b, ln_g, ln_b, layer, e, *, tm, rows):
    bsz, s, d = x.shape
    width = w.shape[2]
    c = width // 5
    tok = lambda b, si: (b, si, 0)
    sel = lambda b, si: (e, 0, 0)
    vec = pl.BlockSpec((None, 1, c), sel)
    blocks = (_nbytes((tm, d), F32) + _nbytes((1, d), F32) + 4 * _nbytes((tm, c), BF16)
              + _nbytes((CONV_WIDTH + 1, c), F32) + 3 * _nbytes((V7X_SUBLANES, c), F32))
    buf_shape = (V7X_SUBLANES, tm + CONV_HALO, c)
    scratch = _nbytes(buf_shape, F32) + _nbytes((CONV_HALO, c), F32) + _nbytes((d, width), BF16)
    temps = 4 * _nbytes((tm, c), F32) + _nbytes((tm, d), BF16)
    return pl.pallas_call(
        functools.partial(_ab_in_body, tm=tm, c=c, rows=rows),
        grid=(bsz, s // tm),
        in_specs=[
            pl.BlockSpec((1, tm, d), tok),
            pl.BlockSpec((None, 1, d), lambda b, si: (layer, 0, 0)),
            _resident((None, d, width), sel),
            pl.BlockSpec((None, CONV_WIDTH, c), sel),
            vec, vec, vec,
        ],
        out_specs=[pl.BlockSpec((1, tm, c), tok)] * 4,
        out_shape=[jax.ShapeDtypeStruct((bsz, s, c), BF16)] * 4,
        scratch_shapes=[pltpu.VMEM(buf_shape, F32), pltpu.VMEM((CONV_HALO, c), F32)],
        compiler_params=_compiler_params(("parallel", "arbitrary"), blocks, scratch, temps),
        name="ab_in",
    )(x, norm, w, conv_w, conv_b, ln_g, ln_b)


SB_DEAD_MASS = 110.0
SB_MASKED_SCORE = -1e30


def _sb_body(q_ref, k_ref, v_ref, tri_ref, o_ref, acc_ref, run_ref, *, seq, chunk):
    tri = tri_ref[...]
    sub = tri.shape[0]

    def tile(r0, rows, s0, keys, diagonal, first):
        q = q_ref[0, r0:r0 + rows, :]
        zero = jnp.zeros_like(q)
        low = lax.broadcasted_iota(jnp.int32, q.shape, 1) < HEAD_DIM
        kblk = k_ref[0, s0:s0 + keys, :]
        vblk = v_ref[0, s0:s0 + keys, :]
        if diagonal:
            row = lax.broadcasted_iota(jnp.int32, (rows, keys), 0) + r0
            col = lax.broadcasted_iota(jnp.int32, (rows, keys), 1) + s0
            strict = col < row
        for hh in range(HEADS_PER_GROUP):
            q_head = jnp.where(low, q, zero) if hh == 0 else jnp.where(low, zero, q)
            z = lax.dot_general(q_head, kblk, _NT, preferred_element_type=F32)
            if diagonal:
                z = jnp.where(strict, z, SB_MASKED_SCORE)
            sp = _softplus(z)
            run = jnp.zeros((rows, V7X_LANES), F32) if first else run_ref[hh, r0:r0 + rows, :]
            parts = [None] * (keys // V7X_LANES)
            for c in reversed(range(keys // sub)):
                sp_c = sp[:, c * sub:(c + 1) * sub]
                after = _dot(sp_c.astype(BF16), tri)
                for g in range(sub // V7X_LANES):
                    lanes = slice(c * sub + g * V7X_LANES, c * sub + (g + 1) * V7X_LANES)
                    w = jnp.exp(z[:, lanes] - sp[:, lanes] - after[:, g * V7X_LANES:(g + 1) * V7X_LANES] - run)
                    parts[lanes.start // V7X_LANES] = w.astype(BF16)
                run = run + jnp.sum(sp_c, axis=1, keepdims=True)
            out = _dot(jnp.concatenate(parts, axis=1), vblk)
            if first:
                acc_ref[hh, r0:r0 + rows, :] = out
            else:
                acc_ref[hh, r0:r0 + rows, :] += out
            run_ref[hh, r0:r0 + rows, :] = run

    n_chunks = seq // chunk
    tile(0, chunk, 0, chunk, True, True)
    for c in range(1, n_chunks):
        tile(c * chunk, chunk, (c - 1) * chunk, 2 * chunk, True, True)

    if n_chunks > 2:
        @pl.when(jnp.min(run_ref[:, 2 * chunk:, :]) < SB_DEAD_MASS)
        def _():
            for kb in reversed(range(n_chunks - 2)):
                r0 = (kb + 2) * chunk
                tile(r0, seq - r0, kb * chunk, chunk, False, False)

    o_ref[0] = jnp.where(lax.broadcasted_iota(jnp.int32, (seq, V7X_LANES), 1) < HEAD_DIM,
                         acc_ref[0], acc_ref[1]).astype(BF16)


def _strict_lower_ones(n):
    j = lax.broadcasted_iota(jnp.int32, (n, n), 0)
    s = lax.broadcasted_iota(jnp.int32, (n, n), 1)
    return (j > s).astype(BF16)


def _sb_attention(q, k, v, *, chunk):
    bsz, s, hd = q.shape
    groups = hd // V7X_LANES
    assert s % chunk == 0
    spec = pl.BlockSpec((1, s, V7X_LANES), lambda b, p: (b, 0, p))
    blocks = 4 * _nbytes((s, V7X_LANES), BF16) + _nbytes((chunk, chunk), BF16)
    state = pltpu.VMEM((HEADS_PER_GROUP, s, V7X_LANES), F32)
    return pl.pallas_call(
        functools.partial(_sb_body, seq=s, chunk=chunk),
        grid=(bsz, groups),
        in_specs=[spec, spec, spec, pl.BlockSpec((chunk, chunk), lambda b, p: (0, 0))],
        out_specs=spec,
        out_shape=jax.ShapeDtypeStruct((bsz, s, hd), BF16),
        scratch_shapes=[state, state],
        compiler_params=_compiler_params(("parallel", "parallel"), blocks,
                                         2 * HEADS_PER_GROUP * _nbytes((s, V7X_LANES), F32),
                                         12 * _nbytes((s, chunk), F32)),
        name="sb_attention",
    )(q, k, v, _strict_lower_ones(chunk))


def _fox_in_body(x_ref, g_ref, w_ref, wft_ref, fb_ref, qg_ref, kg_ref, bd_ref, tri_ref,
                 q_ref, k_ref, v_ref, cum_ref, carry_ref, *, tm, hd):
    s = pl.program_id(1)
    h = _rms_to_bf16(x_ref[0], g_ref[...])
    bd = bd_ref[...]
    chunk = bd.shape[0]
    inv_dim = 1.0 / HEAD_DIM
    for base, gain_ref, out_ref, scale in ((0, qg_ref, q_ref, ATTN_SCALE), (hd, kg_ref, k_ref, 1.0)):
        proj = _dot(h, w_ref[:, base:base + hd])
        for c in range(hd // chunk):
            lanes = slice(c * chunk, (c + 1) * chunk)
            xg = proj[:, lanes]
            ss = _dot((xg * xg).astype(BF16), bd)
            y = (xg * lax.rsqrt(ss * inv_dim + EPS)) * gain_ref[...]
            out_ref[0, :, lanes] = (y * scale).astype(BF16)
    v_ref[0] = _dot(h, w_ref[:, 2 * hd:3 * hd]).astype(BF16)

    xf = lax.dot_general(wft_ref[...], h, _NT, preferred_element_type=F32) + fb_ref[...]
    logf = jnp.minimum(xf, 0.0) - jnp.log(1.0 + jnp.exp(-jnp.abs(xf)))
    p0 = logf.astype(BF16)
    r0 = logf - p0.astype(F32)
    p1 = r0.astype(BF16)
    p2 = (r0 - p1.astype(F32)).astype(BF16)
    tri = tri_ref[...]
    local = _dot(p0, tri) + _dot(p1, tri) + _dot(p2, tri)

    @pl.when(s == 0)
    def _():
        carry_ref[...] = jnp.zeros_like(carry_ref)

    cum = local + carry_ref[...]
    cum_ref[0] = cum
    carry_ref[...] = cum[:, tm - 1:tm]


NORM_CHUNK = V7X_MXU_DIM


def _head_block_ones():
    a = lax.broadcasted_iota(jnp.int32, (NORM_CHUNK, NORM_CHUNK), 0) // HEAD_DIM
    b = lax.broadcasted_iota(jnp.int32, (NORM_CHUNK, NORM_CHUNK), 1) // HEAD_DIM
    return (a == b).astype(BF16)


def _inclusive_upper_ones(n):
    j = lax.broadcasted_iota(jnp.int32, (n, n), 0)
    s = lax.broadcasted_iota(jnp.int32, (n, n), 1)
    return (j <= s).astype(BF16)


def _fox_in(x, norm, w, wft, fb, qg, kg, layer, o, *, tm):
    bsz, s, d = x.shape
    hd = w.shape[2] // 3
    heads = wft.shape[1]
    tok = lambda b, si: (b, si, 0)
    sel = lambda b, si: (o, 0, 0)
    blocks = (_nbytes((tm, d), F32) + _nbytes((d, 3 * hd), BF16) + _nbytes((heads, d), BF16)
              + 3 * _nbytes((tm, hd), BF16) + _nbytes((heads, tm), F32) + _nbytes((tm, tm), BF16)
              + _nbytes((NORM_CHUNK, NORM_CHUNK), BF16) + 4 * _nbytes((V7X_SUBLANES, d), F32))
    temps = _nbytes((tm, d), BF16) + 2 * _nbytes((tm, hd), F32) + 6 * _nbytes((tm, NORM_CHUNK), F32)
    return pl.pallas_call(
        functools.partial(_fox_in_body, tm=tm, hd=hd),
        grid=(bsz, s // tm),
        in_specs=[
            pl.BlockSpec((1, tm, d), tok),
            pl.BlockSpec((None, 1, d), lambda b, si: (layer, 0, 0)),
            pl.BlockSpec((None, d, 3 * hd), sel),
            pl.BlockSpec((None, heads, d), sel),
            pl.BlockSpec((None, heads, 1), sel),
            pl.BlockSpec((None, 1, NORM_CHUNK), sel),
            pl.BlockSpec((None, 1, NORM_CHUNK), sel),
            pl.BlockSpec((NORM_CHUNK, NORM_CHUNK), lambda b, si: (0, 0)),
            pl.BlockSpec((tm, tm), lambda b, si: (0, 0)),
        ],
        out_specs=[pl.BlockSpec((1, tm, hd), tok)] * 3 + [pl.BlockSpec((1, heads, tm), lambda b, si: (b, 0, si))],
        out_shape=[jax.ShapeDtypeStruct((bsz, s, hd), BF16)] * 3 + [jax.ShapeDtypeStruct((bsz, heads, s), F32)],
        scratch_shapes=[pltpu.VMEM((heads, 1), F32)],
        compiler_params=_compiler_params(("parallel", "arbitrary"), blocks,
                                         _nbytes((heads, V7X_LANES), F32), temps),
        name="fox_in",
    )(x, norm, w, wft, fb, qg, kg, _head_block_ones(), _inclusive_upper_ones(tm))


def _fox_body(q_ref, k_ref, v_ref, cum_ref, o_ref, vx_ref, acc_ref, m_ref, *, seq, tk):
    p = pl.program_id(1)
    v = v_ref[0]
    low = lax.broadcasted_iota(jnp.int32, v.shape, 1) < HEAD_DIM
    ones = jnp.ones_like(v)
    vx_ref[0] = jnp.where(low, v, ones)
    vx_ref[1] = jnp.where(low, ones, v)
    acc_ref[...] = jnp.zeros_like(acc_ref)
    m_ref[...] = jnp.full_like(m_ref, -jnp.inf)

    def tile(r0, rows, s0, keys, diagonal):
        q = q_ref[0, r0:r0 + rows, :]
        zero = jnp.zeros_like(q)
        low_q = lax.broadcasted_iota(jnp.int32, q.shape, 1) < HEAD_DIM
        kblk = k_ref[0, s0:s0 + keys, :]
        if diagonal:
            row = lax.broadcasted_iota(jnp.int32, (rows, keys), 0) + r0
            col = lax.broadcasted_iota(jnp.int32, (rows, keys), 1) + s0
            causal = col <= row
        for hh in range(HEADS_PER_GROUP):
            q_head = jnp.where(low_q, q, zero) if hh == 0 else jnp.where(low_q, zero, q)
            decay = cum_ref[0, pl.ds(p * HEADS_PER_GROUP + hh, 1), s0:s0 + keys]
            z = lax.dot_general(q_head, kblk, _NT, preferred_element_type=F32) - decay
            if diagonal:
                z = jnp.where(causal, z, -jnp.inf)
            m_old = m_ref[hh, r0:r0 + rows, :]
            m_new = jnp.maximum(m_old, jnp.max(z, axis=1, keepdims=True))
            alpha = jnp.exp(m_old - m_new)
            pr = jnp.concatenate(
                [jnp.exp(z[:, g * V7X_LANES:(g + 1) * V7X_LANES] - m_new) for g in range(keys // V7X_LANES)], axis=1)
            acc_ref[hh, r0:r0 + rows, :] = (alpha * acc_ref[hh, r0:r0 + rows, :]
                                            + _dot(pr.astype(BF16), vx_ref[hh, s0:s0 + keys, :]))
            m_ref[hh, r0:r0 + rows, :] = m_new

    half = tk // 2
    for base in range(0, seq, tk):
        tile(base, half, base, half, True)
        tile(base + half, seq - base - half, base, tk, True)

    outs = [acc_ref[hh] / pltpu.roll(acc_ref[hh], HEAD_DIM, axis=1) for hh in range(HEADS_PER_GROUP)]
    o_ref[0] = jnp.where(low, outs[0], outs[1]).astype(BF16)


def _fox_attention(q, k, v, cum, *, tk):
    bsz, s, hd = q.shape
    heads = cum.shape[1]
    groups = hd // V7X_LANES
    assert s % tk == 0
    spec = pl.BlockSpec((1, s, V7X_LANES), lambda b, p: (b, 0, p))
    blocks = 4 * _nbytes((s, V7X_LANES), BF16) + _nbytes((heads, s), F32)
    state = pltpu.VMEM((HEADS_PER_GROUP, s, V7X_LANES), F32)
    return pl.pallas_call(
        functools.partial(_fox_body, seq=s, tk=tk),
        grid=(bsz, groups),
        in_specs=[spec, spec, spec, pl.BlockSpec((1, heads, s), lambda b, p: (b, 0, 0))],
        out_specs=spec,
        out_shape=jax.ShapeDtypeStruct((bsz, s, hd), BF16),
        scratch_shapes=[pltpu.VMEM((HEADS_PER_GROUP, s, V7X_LANES), BF16), state, state],
        compiler_params=_compiler_params(("parallel", "parallel"), blocks,
                                         HEADS_PER_GROUP * (_nbytes((s, V7X_LANES), BF16)
                                                            + 2 * _nbytes((s, V7X_LANES), F32)),
                                         12 * _nbytes((s, tk), F32)),
        name="fox_attention",
    )(q, k, v, cum)


FFN_ROWS = 512
CONV_ACC_ROWS = 512
FOX_KEYS = 512
SB_CHUNK = V7X_MXU_DIM


def kernel(x, ffn1_norm, ffn1_w_in, ffn1_w_out, mix_norm, ffn2_norm, ffn2_w_in, ffn2_w_out, ab_w_in, conv_w, conv_b,
           conv_ln_g, conv_ln_b, ab_w_out, fox_w_in, fox_f_bias, fox_q_norm, fox_k_norm, fox_w_out):
    bsz, s, d = x.shape
    depth = ffn1_norm.shape[0]
    n = bsz * s
    row3 = lambda a: a.reshape(a.shape[0], 1, a.shape[1])

    ffn1_norm, ffn2_norm, mix_norm = row3(ffn1_norm), row3(ffn2_norm), row3(mix_norm)
    conv_b, conv_ln_g, conv_ln_b = row3(conv_b), row3(conv_ln_g), row3(conv_ln_b)
    ffn1_w_in, ffn1_w_out = ffn1_w_in.astype(BF16), ffn1_w_out.astype(BF16)
    ffn2_w_in, ffn2_w_out = ffn2_w_in.astype(BF16), ffn2_w_out.astype(BF16)
    ab_w_in, ab_w_out = ab_w_in.astype(BF16), ab_w_out.astype(BF16)
    fox_w_out = fox_w_out.astype(BF16)
    hd = fox_w_out.shape[1]
    fox_w_qkv = fox_w_in[:, :, :3 * hd].astype(BF16)
    fox_w_ft = jnp.swapaxes(fox_w_in[:, :, 3 * hd:], 1, 2).astype(BF16)
    fox_fb = fox_f_bias[:, :, None]
    fox_qg = row3(jnp.tile(fox_q_norm, (1, NORM_CHUNK // HEAD_DIM)))
    fox_kg = row3(jnp.tile(fox_k_norm, (1, NORM_CHUNK // HEAD_DIM)))

    xf = x.reshape(n, d)
    for layer in range(depth):
        xf = _ffn(xf, ffn1_norm, ffn1_w_in, ffn1_w_out, layer, tm=FFN_ROWS)
        if layer % 2 == 0:
            e = layer // 2
            u, q, k, v = _ab_in(xf.reshape(bsz, s, d), mix_norm, ab_w_in, conv_w, conv_b, conv_ln_g, conv_ln_b,
                                layer, e, tm=FFN_ROWS, rows=CONV_ACC_ROWS)
            att = _sb_attention(q, k, v, chunk=SB_CHUNK)
            c = u.shape[2]
            parts, w_mix, mix_index = [u.reshape(n, c), att.reshape(n, c)], ab_w_out, e
        else:
            o = layer // 2
            q, k, v, cum = _fox_in(xf.reshape(bsz, s, d), mix_norm, fox_w_qkv, fox_w_ft, fox_fb, fox_qg, fox_kg,
                                   layer, o, tm=FFN_ROWS)
            att = _fox_attention(q, k, v, cum, tk=FOX_KEYS)
            parts, w_mix, mix_index = [att.reshape(n, hd)], fox_w_out, o
        xf = _ffn(xf, ffn2_norm, ffn2_w_in, ffn2_w_out, layer, tm=FFN_ROWS,
                  parts=parts, w_mix=w_mix, mix_index=mix_index)
    return xf.reshape(bsz, s, d)
```
